```python
import jax, jax.numpy as jnp
from jax import lax
import numpy as np

D_MODEL = 1024
BATCH = 4
SEQ = 8192
DEPTH = 1

CHUNK = 64
Q_BLOCK = 128
PLE_DIM = 256
GLA_HEADS = 4
GLA_DK = D_MODEL // 8
GLA_DV = D_MODEL // 4
GLA_LOWRANK = 16
GLA_TAU = 16.0
SB_HEADS = 8
SB_DH = D_MODEL // 8
GLA_QK = GLA_HEADS * GLA_DK
GLA_V = GLA_HEADS * GLA_DV
SB_W = SB_HEADS * SB_DH
EPS = 1e-6

kernel_name = "hybrid_gla_stickbreak_gated_block"


def _split_sizes():
    return [GLA_QK, GLA_QK, GLA_V, GLA_LOWRANK, GLA_V, SB_W, SB_W, SB_W, SB_W, D_MODEL, D_MODEL]


def _d_in():
    return sum(_split_sizes())


def rmsnorm(x, g):
    xf = x.astype(jnp.float32)
    y = xf * lax.rsqrt(jnp.mean(xf * xf, axis=-1, keepdims=True) + EPS)
    return (y * g.astype(jnp.float32)).astype(x.dtype)


def gla_mixer(q, k, v, log_a):
    B, S, H, dk = q.shape
    dv = v.shape[-1]
    n = S // CHUNK

    def to_chunks(t):
        return t.reshape(B, n, CHUNK, H, t.shape[-1]).transpose(1, 0, 3, 2, 4)

    qc = to_chunks(q.astype(jnp.float32) * (dk ** -0.5))
    kc = to_chunks(k.astype(jnp.float32))
    vc = to_chunks(v.astype(jnp.float32))
    bc = jnp.cumsum(to_chunks(log_a.astype(jnp.float32)), axis=3)
    causal = jnp.tril(jnp.ones((CHUNK, CHUNK), dtype=bool))

    def step(state, inp):
        q_, k_, v_, b_ = inp
        eb, enb = jnp.exp(b_), jnp.exp(-b_)
        qd, kd = q_ * eb, k_ * enb
        qi, ki = q_ * enb, k_ * eb
        a_past = jnp.einsum('bhtd,bhsd->bhts', qd, kd)
        a_fut = jnp.einsum('bhtd,bhsd->bhts', qi, ki)
        attn = jnp.where(causal, a_past, a_fut)
        o = jnp.einsum('bhts,bhsv->bhtv', attn, v_) + jnp.einsum('bhtd,bhdv->bhtv', qd, state)
        b_last = b_[:, :, -1:, :]
        k_end = k_ * jnp.exp(b_last - b_)
        state = state * jnp.exp(b_last)[:, :, 0, :, None] + jnp.einsum('bhsd,bhsv->bhdv', k_end, v_)
        return state, o

    state0 = jnp.zeros((B, H, dk, dv), jnp.float32)
    _, o = lax.scan(step, state0, (qc, kc, vc, bc))
    return o.transpose(1, 0, 3, 2, 4).reshape(B, S, H, dv)


def stick_breaking(q, k, v):
    B, S, H, d = q.shape
    nb = S // Q_BLOCK
    qb = q.reshape(B, nb, Q_BLOCK, H, d).transpose(1, 0, 3, 2, 4)
    kt = k.transpose(0, 2, 1, 3)
    vt = v.transpose(0, 2, 1, 3)
    key_pos = jnp.arange(S)
    scale = d ** -0.5

    def block(args):
        qblk, i = args
        z = jnp.einsum('bhtd,bhsd->bhts', qblk, kt).astype(jnp.float32) * scale
        q_pos = i * Q_BLOCK + jnp.arange(Q_BLOCK)
        mask = key_pos[None, :] < q_pos[:, None]
        log_1m = jnp.where(mask, jax.nn.log_sigmoid(-z), 0.0)
        after = lax.cumsum(log_1m, axis=3, reverse=True) - log_1m
        w = jnp.where(mask, jnp.exp(jax.nn.log_sigmoid(z) + after), 0.0)
        return jnp.einsum('bhts,bhsd->bhtd', w.astype(vt.dtype), vt)

    o = lax.map(block, (qb, jnp.arange(nb)))
    return o.transpose(1, 0, 3, 2, 4).reshape(B, S, H, d)


def setup_inputs(seed: int = 0) -> dict:
    key = jax.random.key(seed)
    ks = jax.random.split(key, 16)
    d_in = _d_in()
    f32 = jnp.float32

    def nrm(k, shape, fan_in):
        return jax.random.normal(k, shape, f32) * (fan_in ** -0.5)

    def gain(k, shape):
        return 1.0 + 0.05 * jax.random.normal(k, shape, f32)

    return {
        "x": jax.random.normal(ks[0], (BATCH, SEQ, D_MODEL), f32),
        "p": jax.random.normal(ks[1], (DEPTH, BATCH, SEQ, PLE_DIM), f32),
        "g_mix": gain(ks[2], (DEPTH, D_MODEL)),
        "w_in": nrm(ks[3], (DEPTH, D_MODEL, d_in), D_MODEL),
        "w_alpha": nrm(ks[4], (DEPTH, GLA_LOWRANK, GLA_QK), GLA_LOWRANK),
        "b_alpha": 0.5 + 0.1 * jax.random.normal(ks[5], (DEPTH, GLA_QK), f32),
        "g_gla_out": gain(ks[6], (DEPTH, GLA_HEADS, GLA_DV)),
        "w_out_gla": nrm(ks[7], (DEPTH, GLA_V, D_MODEL), GLA_V),
        "w_out_sb": nrm(ks[8], (DEPTH, SB_W, D_MODEL), SB_W),
        "w_out": nrm(ks[9], (DEPTH, D_MODEL, D_MODEL), D_MODEL),
        "g_ple": gain(ks[10], (DEPTH, D_MODEL)),
        "w_ple_gate": nrm(ks[11], (DEPTH, D_MODEL, D_MODEL), D_MODEL),
        "w_ple": nrm(ks[12], (DEPTH, PLE_DIM, D_MODEL), PLE_DIM),
        "g_final": gain(ks[13], (D_MODEL,)),
    }


def reference(x, p, g_mix, w_in, w_alpha, b_alpha, g_gla_out, w_out_gla, w_out_sb, w_out,
              g_ple, w_ple_gate, w_ple, g_final):
    B, S, _ = x.shape
    sizes = _split_sizes()
    points = []
    acc = 0
    for s_ in sizes[:-1]:
        acc += s_
        points.append(acc)
    for i in range(DEPTH):
        h = rmsnorm(x, g_mix[i])
        z = jnp.einsum('bsd,de->bse', h, w_in[i])
        (q_a, k_a, v_a, lr_a, gate_a, q_b, k_b, v_b, gate_b, m_a, m_b) = jnp.split(z, points, axis=-1)

        log_a = jax.nn.log_sigmoid(
            (jnp.einsum('bsr,re->bse', lr_a, w_alpha[i]) + b_alpha[i]).astype(jnp.float32)) / GLA_TAU
        o_a = gla_mixer(q_a.reshape(B, S, GLA_HEADS, GLA_DK),
                        k_a.reshape(B, S, GLA_HEADS, GLA_DK),
                        v_a.reshape(B, S, GLA_HEADS, GLA_DV),
                        log_a.reshape(B, S, GLA_HEADS, GLA_DK))
        o_a = rmsnorm(o_a, g_gla_out[i]).astype(x.dtype).reshape(B, S, GLA_V)
        y_a = jnp.einsum('bse,ed->bsd', o_a * jax.nn.silu(gate_a), w_out_gla[i])

        o_b = stick_breaking(q_b.reshape(B, S, SB_HEADS, SB_DH),
                             k_b.reshape(B, S, SB_HEADS, SB_DH),
                             v_b.reshape(B, S, SB_HEADS, SB_DH)).reshape(B, S, SB_W)
        y_b = jnp.einsum('bse,ed->bsd', o_b * jax.nn.silu(gate_b), w_out_sb[i])

        merged = jax.nn.sigmoid(m_a) * y_a + jax.nn.sigmoid(m_b) * y_b
        x = x + jnp.einsum('bsd,de->bse', merged, w_out[i])

        u = rmsnorm(x, g_ple[i])
        gate_p = jax.nn.sigmoid(jnp.einsum('bsd,de->bse', u, w_ple_gate[i]))
        x = x + gate_p * jnp.einsum('bsp,pd->bsd', p[i], w_ple[i])
    return rmsnorm(x, g_final)
```

```python
import functools

import jax
import jax.numpy as jnp
from jax import lax
from jax.experimental import pallas as pl
from jax.experimental.pallas import tpu as pltpu

D_MODEL = 1024
CHUNK = 64
PLE_DIM = 256
GLA_HEADS = 4
GLA_DK = 128
GLA_DV = 256
GLA_LOWRANK = 16
GLA_TAU = 16.0
SB_HEADS = 8
SB_DH = 128
GLA_QK = GLA_HEADS * GLA_DK
GLA_V = GLA_HEADS * GLA_DV
SB_W = SB_HEADS * SB_DH
EPS = 1e-6

LANE = 128

OFF_QK_A = 0
OFF_V_A = OFF_QK_A + 2 * GLA_QK
OFF_GATE_A = OFF_V_A + GLA_V
OFF_Q_B = OFF_GATE_A + GLA_V
OFF_K_B = OFF_Q_B + SB_W
OFF_V_B = OFF_K_B + SB_W
OFF_GATE_B = OFF_V_B + SB_W
OFF_M_A = OFF_GATE_B + SB_W
OFF_M_B = OFF_M_A + D_MODEL
Z_COLS = OFF_M_B + D_MODEL

SB_DEAD_LOG_WEIGHT = -104.0

IN_PROJ_ROWS = 512
IN_PROJ_COLS = 1024
GLA_ROWS = 256
SB_Q_BLOCK = 128
SB_K_BLOCK = 128
OUT_PROJ_ROWS = 512

VMEM_LIMIT_BYTES = 56 * 1024 * 1024

BF16 = jnp.bfloat16
F32 = jnp.float32


def _dot(a, b):
    return jnp.dot(a, b, preferred_element_type=F32)


def _dot_nt(a, b):
    return lax.dot_general(a, b, (((1,), (1,)), ((), ())), preferred_element_type=F32)


def _dot_tn(a, b):
    return lax.dot_general(a, b, (((0,), (0,)), ((), ())), preferred_element_type=F32)


def _dot_f32(a, b):
    a_hi = a.astype(BF16)
    a_lo = (a - a_hi.astype(F32)).astype(BF16)
    b_hi = b.astype(BF16)
    b_lo = (b - b_hi.astype(F32)).astype(BF16)
    return _dot(a_hi, b_hi) + (_dot(a_lo, b_hi) + _dot(a_hi, b_lo))


def _dot_f32_exact_rhs(a, b_bf16):
    a_hi = a.astype(BF16)
    a_lo = (a - a_hi.astype(F32)).astype(BF16)
    return _dot(a_hi, b_bf16) + _dot(a_lo, b_bf16)


def _rms_scale(x):
    return lax.rsqrt(jnp.mean(x * x, axis=-1, keepdims=True) + EPS)


def _log_sigmoid_neg(s):
    return -(jnp.maximum(s, 0.0) + jnp.log(1.0 + jnp.exp(-jnp.abs(s))))


def _silu(g):
    return g * (1.0 / (1.0 + jnp.exp(-g)))


def _sigmoid(g):
    return 1.0 / (1.0 + jnp.exp(-g))


def _in_proj_kernel(x_ref, g_ref, w_ref, wlr_ref, z_ref, lr_ref):
    x = x_ref[...]
    h = (x * _rms_scale(x) * g_ref[...]).astype(BF16)
    for c in range(0, Z_COLS, IN_PROJ_COLS):
        z_ref[:, c:c + IN_PROJ_COLS] = _dot(h, w_ref[:, c:c + IN_PROJ_COLS]).astype(BF16)
    lr_ref[...] = _dot(h, wlr_ref[...])


def _in_proj(x2, g_mix, w_main, w_lr):
    t = x2.shape[0]
    return pl.pallas_call(
        _in_proj_kernel,
        grid=(t // IN_PROJ_ROWS,),
        in_specs=[
            pl.BlockSpec((IN_PROJ_ROWS, D_MODEL), lambda i: (i, 0)),
            pl.BlockSpec((1, D_MODEL), lambda i: (0, 0)),
            pl.BlockSpec((D_MODEL, Z_COLS), lambda i: (0, 0), pipeline_mode=pl.Buffered(1)),
            pl.BlockSpec((D_MODEL, LANE), lambda i: (0, 0)),
        ],
        out_specs=[
            pl.BlockSpec((IN_PROJ_ROWS, Z_COLS), lambda i: (i, 0)),
            pl.BlockSpec((IN_PROJ_ROWS, LANE), lambda i: (i, 0)),
        ],
        out_shape=[
            jax.ShapeDtypeStruct((t, Z_COLS), BF16),
            jax.ShapeDtypeStruct((t, LANE), F32),
        ],
        compiler_params=pltpu.CompilerParams(
            dimension_semantics=("arbitrary",), vmem_limit_bytes=VMEM_LIMIT_BYTES),
        name="in_proj",
    )(x2, g_mix, w_main, w_lr)


def _gla_kernel(qk_ref, v_ref, gate_ref, lr_ref, walpha_ref, balpha_ref, gout_ref, o_ref, state_ref):
    rows = GLA_ROWS
    n_chunks = rows // CHUNK

    @pl.when(pl.program_id(1) == 0)
    def _():
        state_ref[...] = jnp.zeros_like(state_ref)

    r_id = lax.broadcasted_iota(jnp.int32, (rows, rows), 0)
    c_id = lax.broadcasted_iota(jnp.int32, (rows, rows), 1)
    chunk_shift = CHUNK.bit_length() - 1
    same_chunk = jnp.right_shift(r_id, chunk_shift) == jnp.right_shift(c_id, chunk_shift)
    causal = r_id >= c_id
    cum_mat = jnp.where(same_chunk & causal, 1.0, 0.0).astype(BF16)
    tot_mat = jnp.where(same_chunk, 1.0, 0.0).astype(BF16)

    logits = _dot_f32(lr_ref[...], walpha_ref[...]) + balpha_ref[...]
    log_a = (jnp.minimum(logits, 0.0) - jnp.log(1.0 + jnp.exp(-jnp.abs(logits)))) * (1.0 / GLA_TAU)
    b_all = _dot_f32_exact_rhs_left(cum_mat, log_a)
    bl_all = _dot_f32_exact_rhs_left(tot_mat, log_a)

    for h in range(GLA_HEADS):
        ks = slice(h * GLA_DK, (h + 1) * GLA_DK)
        vs = slice(h * GLA_DV, (h + 1) * GLA_DV)
        q = qk_ref[:, h * GLA_DK:(h + 1) * GLA_DK].astype(F32) * (GLA_DK ** -0.5)
        k = qk_ref[:, GLA_QK + h * GLA_DK:GLA_QK + (h + 1) * GLA_DK].astype(F32)
        v = v_ref[:, vs]
        b = b_all[:, ks]
        bl = bl_all[:, ks]
        eb = jnp.exp(b)
        enb = jnp.exp(-b)
        qd = (q * eb).astype(BF16)
        kd = (k * enb).astype(BF16)
        qi = (q * enb).astype(BF16)
        ki = (k * eb).astype(BF16)
        k_end = (k * jnp.exp(bl - b)).astype(BF16)
        decay = jnp.exp(bl)

        a_past = _dot_nt(qd, kd)
        a_fut = _dot_nt(qi, ki)
        attn = jnp.where(same_chunk, jnp.where(causal, a_past, a_fut), 0.0).astype(BF16)
        o_intra = _dot(attn, v)

        state = state_ref[h]
        o_inter = []
        for c in range(n_chunks):
            rs = slice(c * CHUNK, (c + 1) * CHUNK)
            o_inter.append(_dot_nt(qd[rs], state.astype(BF16)))
            state = state * decay[c * CHUNK:c * CHUNK + 1, :] + _dot_tn(v[rs], k_end[rs])
        state_ref[h] = state
        o = o_intra + jnp.concatenate(o_inter, axis=0)

        o = o * _rms_scale(o) * gout_ref[:, vs]
        o_ref[:, vs] = (o * _silu(gate_ref[:, vs].astype(F32))).astype(BF16)


def _dot_f32_exact_rhs_left(m_bf16, a):
    a_hi = a.astype(BF16)
    a_lo = (a - a_hi.astype(F32)).astype(BF16)
    return _dot(m_bf16, a_hi) + _dot(m_bf16, a_lo)


def _gla(z3, lr3, w_alpha, b_alpha, g_out):
    bsz, seq, _ = z3.shape
    nblk = D_MODEL
    return pl.pallas_call(
        _gla_kernel,
        grid=(bsz, seq // GLA_ROWS),
        in_specs=[
            pl.BlockSpec((None, GLA_ROWS, 2 * GLA_QK), lambda b, s: (b, s, OFF_QK_A // nblk)),
            pl.BlockSpec((None, GLA_ROWS, GLA_V), lambda b, s: (b, s, OFF_V_A // nblk)),
            pl.BlockSpec((None, GLA_ROWS, GLA_V), lambda b, s: (b, s, OFF_GATE_A // nblk)),
            pl.BlockSpec((None, GLA_ROWS, LANE), lambda b, s: (b, s, 0)),
            pl.BlockSpec((LANE, GLA_QK), lambda b, s: (0, 0)),
            pl.BlockSpec((1, GLA_QK), lambda b, s: (0, 0)),
            pl.BlockSpec((1, GLA_V), lambda b, s: (0, 0)),
        ],
        out_specs=pl.BlockSpec((None, GLA_ROWS, GLA_V), lambda b, s: (b, s, 0)),
        out_shape=jax.ShapeDtypeStruct((bsz, seq, GLA_V), BF16),
        scratch_shapes=[pltpu.VMEM((GLA_HEADS, GLA_DV, GLA_DK), F32)],
        compiler_params=pltpu.CompilerParams(
            dimension_semantics=("arbitrary", "arbitrary"), vmem_limit_bytes=VMEM_LIMIT_BYTES),
        name="gla",
    )(z3, z3, z3, lr3, w_alpha, b_alpha, g_out)


def _sb_kernel(q_ref, k_ref, v_ref, gate_ref, o_ref):
    seq = q_ref.shape[0]
    tq, tk = SB_Q_BLOCK, SB_K_BLOCK
    scale = SB_DH ** -0.5
    r_id = lax.broadcasted_iota(jnp.int32, (tq, tk), 0)
    c_id = lax.broadcasted_iota(jnp.int32, (tq, tk), 1)
    strictly_before = c_id < r_id
    later_keys = jnp.where(r_id > c_id, 1.0, 0.0).astype(BF16)

    def key_block(q, j, carry, acc, on_diagonal):
        k0 = pl.multiple_of(j * tk, tk)
        k = k_ref[pl.ds(k0, tk), :]
        v = v_ref[pl.ds(k0, tk), :]
        s = _dot_nt(q, k) * scale
        log_1m = _log_sigmoid_neg(s)
        log_beta = log_1m + s
        if on_diagonal:
            log_1m = jnp.where(strictly_before, log_1m, 0.0)
        after = _dot_f32_exact_rhs(log_1m, later_keys) + carry
        w = jnp.exp(log_beta + after)
        if on_diagonal:
            w = jnp.where(strictly_before, w, 0.0)
        acc = acc + _dot(w.astype(BF16), v)
        carry = carry + jnp.sum(log_1m, axis=1, keepdims=True)
        return carry, acc

    def q_block(i, _):
        q0 = pl.multiple_of(i * tq, tq)
        q = q_ref[pl.ds(q0, tq), :]
        carry = jnp.zeros((tq, 1), F32)
        acc = jnp.zeros((tq, SB_DH), F32)
        carry, acc = key_block(q, i, carry, acc, True)

        def cond(st):
            j, carry, _ = st
            return jnp.logical_and(j >= 0, jnp.max(carry) > SB_DEAD_LOG_WEIGHT)

        def body(st):
            j, carry, acc = st
            carry, acc = key_block(q, j, carry, acc, False)
            return j - 1, carry, acc

        _, _, acc = lax.while_loop(cond, body, (i - 1, carry, acc))
        gate = gate_ref[pl.ds(q0, tq), :].astype(F32)
        o_ref[pl.ds(q0, tq), :] = (acc * _silu(gate)).astype(BF16)
        return 0

    lax.fori_loop(0, seq // tq, q_block, 0)


def _stick_breaking(z3):
    bsz, seq, _ = z3.shape

    def spec(off):
        return pl.BlockSpec((None, seq, SB_DH), lambda b, h: (b, 0, off // SB_DH + h))

    return pl.pallas_call(
        _sb_kernel,
        grid=(bsz, SB_HEADS),
        in_specs=[spec(OFF_Q_B), spec(OFF_K_B), spec(OFF_V_B), spec(OFF_GATE_B)],
        out_specs=pl.BlockSpec((None, seq, SB_DH), lambda b, h: (b, 0, h)),
        out_shape=jax.ShapeDtypeStruct((bsz, seq, SB_W), BF16),
        compiler_params=pltpu.CompilerParams(
            dimension_semantics=("arbitrary", "arbitrary"), vmem_limit_bytes=VMEM_LIMIT_BYTES),
        name="stickbrk",
    )(z3, z3, z3, z3)


def _out_proj_kernel(ga_ref, gb_ref, ma_ref, mb_ref, x_ref, p_ref, wa_ref, wb_ref, wo_ref, wg_ref, wp_ref,
                     gple_ref, gfin_ref, o_ref):
    y_a = _dot(ga_ref[...], wa_ref[...])
    y_b = _dot(gb_ref[...], wb_ref[...])
    merged = _sigmoid(ma_ref[...].astype(F32)) * y_a + _sigmoid(mb_ref[...].astype(F32)) * y_b
    x1 = x_ref[...] + _dot(merged.astype(BF16), wo_ref[...])
    u = (x1 * _rms_scale(x1) * gple_ref[...]).astype(BF16)
    gate_p = _sigmoid(_dot(u, wg_ref[...]))
    x2 = x1 + gate_p * _dot(p_ref[...].astype(BF16), wp_ref[...])
    o_ref[...] = x2 * _rms_scale(x2) * gfin_ref[...]


def _out_proj(ga, gb, z2, x2, p2, wa, wb, wo, wg, wp, g_ple, g_final):
    t = x2.shape[0]
    rows = OUT_PROJ_ROWS

    def row_spec(width, col_block=0):
        return pl.BlockSpec((rows, width), lambda i: (i, col_block))

    def full_spec(shape):
        return pl.BlockSpec(shape, lambda i: (0, 0))

    return pl.pallas_call(
        _out_proj_kernel,
        grid=(t // rows,),
        in_specs=[
            row_spec(GLA_V), row_spec(SB_W),
            row_spec(D_MODEL, OFF_M_A // D_MODEL), row_spec(D_MODEL, OFF_M_B // D_MODEL),
            row_spec(D_MODEL), row_spec(PLE_DIM),
            full_spec((GLA_V, D_MODEL)), full_spec((SB_W, D_MODEL)), full_spec((D_MODEL, D_MODEL)),
            full_spec((D_MODEL, D_MODEL)), full_spec((PLE_DIM, D_MODEL)),
            full_spec((1, D_MODEL)), full_spec((1, D_MODEL)),
        ],
        out_specs=row_spec(D_MODEL),
        out_shape=jax.ShapeDtypeStruct((t, D_MODEL), F32),
        compiler_params=pltpu.CompilerParams(
            dimension_semantics=("arbitrary",), vmem_limit_bytes=VMEM_LIMIT_BYTES),
        name="out_proj",
    )(ga, gb, z2, z2, x2, p2, wa, wb, wo, wg, wp, g_ple, g_final)


def _regroup_w_in(w):
    q_a, k_a, v_a, lr, gate_a, q_b, k_b, v_b, gate_b, m_a, m_b = jnp.split(
        w, [GLA_QK, 2 * GLA_QK, 2 * GLA_QK + GLA_V, 2 * GLA_QK + GLA_V + GLA_LOWRANK,
            2 * GLA_QK + 2 * GLA_V + GLA_LOWRANK,
            2 * GLA_QK + 2 * GLA_V + GLA_LOWRANK + SB_W,
            2 * GLA_QK + 2 * GLA_V + GLA_LOWRANK + 2 * SB_W,
            2 * GLA_QK + 2 * GLA_V + GLA_LOWRANK + 3 * SB_W,
            2 * GLA_QK + 2 * GLA_V + GLA_LOWRANK + 4 * SB_W,
            2 * GLA_QK + 2 * GLA_V + GLA_LOWRANK + 4 * SB_W + D_MODEL], axis=1)
    main = jnp.concatenate([q_a, k_a, v_a, gate_a, q_b, k_b, v_b, gate_b, m_a, m_b], axis=1).astype(BF16)
    lr = jnp.pad(lr, ((0, 0), (0, LANE - GLA_LOWRANK))).astype(BF16)
    return main, lr


def _layer(x, p, g_mix, w_in, w_alpha, b_alpha, g_gla_out, w_out_gla, w_out_sb, w_out, g_ple, w_ple_gate,
           w_ple, g_last):
    bsz, seq, _ = x.shape
    t = bsz * seq
    x2 = x.reshape(t, D_MODEL)
    w_main, w_lr = _regroup_w_in(w_in)
    z, lr = _in_proj(x2, g_mix.reshape(1, D_MODEL), w_main, w_lr)
    z3 = z.reshape(bsz, seq, Z_COLS)
    w_alpha_p = jnp.pad(w_alpha, ((0, LANE - GLA_LOWRANK), (0, 0)))
    ga = _gla(z3, lr.reshape(bsz, seq, LANE), w_alpha_p, b_alpha.reshape(1, GLA_QK),
              g_gla_out.reshape(1, GLA_V))
    gb = _stick_breaking(z3)
    out = _out_proj(ga.reshape(t, GLA_V), gb.reshape(t, SB_W), z, x2, p.reshape(t, PLE_DIM),
                    w_out_gla.astype(BF16), w_out_sb.astype(BF16), w_out.astype(BF16),
                    w_ple_gate.astype(BF16), w_ple.astype(BF16),
                    g_ple.reshape(1, D_MODEL), g_last.reshape(1, D_MODEL))
    return out.reshape(bsz, seq, D_MODEL)


def kernel(x, p, g_mix, w_in, w_alpha, b_alpha, g_gla_out, w_out_gla, w_out_sb, w_out, g_ple, w_ple_gate, w_ple,
           g_final):
    depth = p.shape[0]
    assert depth == 1, "the fused final RMSNorm assumes a single layer"
    return _layer(x, p[0], g_mix[0], w_in[0], w_alpha[0], b_alpha[0], g_gla_out[0], w_out_gla[0], w_out_sb[0],
                  w_out[0], g_ple[0], w_ple_gate[0], w_ple[0], g_final)
```

```python
import functools

import jax
import jax.numpy as jnp
from jax import lax
from jax.experimental import pallas as pl
from jax.experimental.pallas import tpu as pltpu

D_MODEL = 1024
CHUNK = 64
PLE_DIM = 256
GLA_HEADS = 4
GLA_DK = 128
GLA_DV = 256
GLA_LOWRANK = 16
GLA_TAU = 16.0
SB_HEADS = 8
SB_DH = 128
GLA_QK = GLA_HEADS * GLA_DK
GLA_V = GLA_HEADS * GLA_DV
SB_W = SB_HEADS * SB_DH
EPS = 1e-6

LANE = 128

OFF_QK_A = 0
OFF_V_A = OFF_QK_A + 2 * GLA_QK
OFF_GATE_A = OFF_V_A + GLA_V
OFF_Q_B = OFF_GATE_A + GLA_V
OFF_K_B = OFF_Q_B + SB_W
OFF_V_B = OFF_K_B + SB_W
OFF_GATE_B = OFF_V_B + SB_W
OFF_M_A = OFF_GATE_B + SB_W
OFF_M_B = OFF_M_A + D_MODEL
Z_COLS = OFF_M_B + D_MODEL

SB_DEAD_LOG_WEIGHT = -104.0
SB_NO_KEYS_LOG_WEIGHT = -1e30
SB_STREAMS = 8

IN_PROJ_ROWS = 512
IN_PROJ_COLS = 1024
GLA_ROWS = 256
SB_Q_BLOCK = 128
SB_K_BLOCK = 128
OUT_PROJ_ROWS = 512

VMEM_LIMIT_BYTES = 56 * 1024 * 1024

BF16 = jnp.bfloat16
F32 = jnp.float32


def _dot(a, b):
    return jnp.dot(a, b, preferred_element_type=F32)


def _dot_nt(a, b):
    return lax.dot_general(a, b, (((1,), (1,)), ((), ())), preferred_element_type=F32)


def _dot_tn(a, b):
    return lax.dot_general(a, b, (((0,), (0,)), ((), ())), preferred_element_type=F32)


def _dot_f32(a, b):
    a_hi = a.astype(BF16)
    a_lo = (a - a_hi.astype(F32)).astype(BF16)
    b_hi = b.astype(BF16)
    b_lo = (b - b_hi.astype(F32)).astype(BF16)
    return _dot(a_hi, b_hi) + (_dot(a_lo, b_hi) + _dot(a_hi, b_lo))


def _dot_f32_exact_rhs(a, b_bf16):
    a_hi = a.astype(BF16)
    a_lo = (a - a_hi.astype(F32)).astype(BF16)
    return _dot(a_hi, b_bf16) + _dot(a_lo, b_bf16)


def _rms_scale(x):
    return lax.rsqrt(jnp.mean(x * x, axis=-1, keepdims=True) + EPS)


def _log_sigmoid_neg(s):
    return -(jnp.maximum(s, 0.0) + jnp.log(1.0 + jnp.exp(-jnp.abs(s))))


def _silu(g):
    return g * (1.0 / (1.0 + jnp.exp(-g)))


def _sigmoid(g):
    return 1.0 / (1.0 + jnp.exp(-g))


def _in_proj_kernel(x_ref, g_ref, w_ref, wlr_ref, z_ref, lr_ref):
    x = x_ref[...]
    h = (x * _rms_scale(x) * g_ref[...]).astype(BF16)
    for c in range(0, Z_COLS, IN_PROJ_COLS):
        z_ref[:, c:c + IN_PROJ_COLS] = _dot(h, w_ref[:, c:c + IN_PROJ_COLS]).astype(BF16)
    lr_ref[...] = _dot(h, wlr_ref[...])


def _in_proj(x2, g_mix, w_main, w_lr):
    t = x2.shape[0]
    return pl.pallas_call(
        _in_proj_kernel,
        grid=(t // IN_PROJ_ROWS,),
        in_specs=[
            pl.BlockSpec((IN_PROJ_ROWS, D_MODEL), lambda i: (i, 0)),
            pl.BlockSpec((1, D_MODEL), lambda i: (0, 0)),
            pl.BlockSpec((D_MODEL, Z_COLS), lambda i: (0, 0), pipeline_mode=pl.Buffered(1)),
            pl.BlockSpec((D_MODEL, LANE), lambda i: (0, 0)),
        ],
        out_specs=[
            pl.BlockSpec((IN_PROJ_ROWS, Z_COLS), lambda i: (i, 0)),
            pl.BlockSpec((IN_PROJ_ROWS, LANE), lambda i: (i, 0)),
        ],
        out_shape=[
            jax.ShapeDtypeStruct((t, Z_COLS), BF16),
            jax.ShapeDtypeStruct((t, LANE), F32),
        ],
        compiler_params=pltpu.CompilerParams(
            dimension_semantics=("arbitrary",), vmem_limit_bytes=VMEM_LIMIT_BYTES),
        name="in_proj",
    )(x2, g_mix, w_main, w_lr)


def _gla_kernel(qk_ref, v_ref, gate_ref, lr_ref, walpha_ref, balpha_ref, gout_ref, o_ref, state_ref):
    rows = GLA_ROWS
    n_chunks = rows // CHUNK

    @pl.when(pl.program_id(1) == 0)
    def _():
        state_ref[...] = jnp.zeros_like(state_ref)

    r_id = lax.broadcasted_iota(jnp.int32, (rows, rows), 0)
    c_id = lax.broadcasted_iota(jnp.int32, (rows, rows), 1)
    chunk_shift = CHUNK.bit_length() - 1
    same_chunk = jnp.right_shift(r_id, chunk_shift) == jnp.right_shift(c_id, chunk_shift)
    causal = r_id >= c_id
    cum_mat = jnp.where(same_chunk & causal, 1.0, 0.0).astype(BF16)
    tot_mat = jnp.where(same_chunk, 1.0, 0.0).astype(BF16)

    logits = _dot_f32(lr_ref[...], walpha_ref[...]) + balpha_ref[...]
    log_a = (jnp.minimum(logits, 0.0) - jnp.log(1.0 + jnp.exp(-jnp.abs(logits)))) * (1.0 / GLA_TAU)
    b_all = _dot_f32_exact_rhs_left(cum_mat, log_a)
    bl_all = _dot_f32_exact_rhs_left(tot_mat, log_a)

    for h in range(GLA_HEADS):
        ks = slice(h * GLA_DK, (h + 1) * GLA_DK)
        vs = slice(h * GLA_DV, (h + 1) * GLA_DV)
        q = qk_ref[:, h * GLA_DK:(h + 1) * GLA_DK].astype(F32) * (GLA_DK ** -0.5)
        k = qk_ref[:, GLA_QK + h * GLA_DK:GLA_QK + (h + 1) * GLA_DK].astype(F32)
        v = v_ref[:, vs]
        b = b_all[:, ks]
        bl = bl_all[:, ks]
        eb = jnp.exp(b)
        enb = jnp.exp(-b)
        qd = (q * eb).astype(BF16)
        kd = (k * enb).astype(BF16)
        qi = (q * enb).astype(BF16)
        ki = (k * eb).astype(BF16)
        k_end = (k * jnp.exp(bl - b)).astype(BF16)
        decay = jnp.exp(bl)

        a_past = _dot_nt(qd, kd)
        a_fut = _dot_nt(qi, ki)
        attn = jnp.where(same_chunk, jnp.where(causal, a_past, a_fut), 0.0).astype(BF16)
        o_intra = _dot(attn, v)

        state = state_ref[h]
        o_inter = []
        for c in range(n_chunks):
            rs = slice(c * CHUNK, (c + 1) * CHUNK)
            o_inter.append(_dot_nt(qd[rs], state.astype(BF16)))
            state = state * decay[c * CHUNK:c * CHUNK + 1, :] + _dot_tn(v[rs], k_end[rs])
        state_ref[h] = state
        o = o_intra + jnp.concatenate(o_inter, axis=0)

        o = o * _rms_scale(o) * gout_ref[:, vs]
        o_ref[:, vs] = (o * _silu(gate_ref[:, vs].astype(F32))).astype(BF16)


def _dot_f32_exact_rhs_left(m_bf16, a):
    a_hi = a.astype(BF16)
    a_lo = (a - a_hi.astype(F32)).astype(BF16)
    return _dot(m_bf16, a_hi) + _dot(m_bf16, a_lo)


def _gla(z3, lr3, w_alpha, b_alpha, g_out):
    bsz, seq, _ = z3.shape
    nblk = D_MODEL
    return pl.pallas_call(
        _gla_kernel,
        grid=(bsz, seq // GLA_ROWS),
        in_specs=[
            pl.BlockSpec((None, GLA_ROWS, 2 * GLA_QK), lambda b, s: (b, s, OFF_QK_A // nblk)),
            pl.BlockSpec((None, GLA_ROWS, GLA_V), lambda b, s: (b, s, OFF_V_A // nblk)),
            pl.BlockSpec((None, GLA_ROWS, GLA_V), lambda b, s: (b, s, OFF_GATE_A // nblk)),
            pl.BlockSpec((None, GLA_ROWS, LANE), lambda b, s: (b, s, 0)),
            pl.BlockSpec((LANE, GLA_QK), lambda b, s: (0, 0)),
            pl.BlockSpec((1, GLA_QK), lambda b, s: (0, 0)),
            pl.BlockSpec((1, GLA_V), lambda b, s: (0, 0)),
        ],
        out_specs=pl.BlockSpec((None, GLA_ROWS, GLA_V), lambda b, s: (b, s, 0)),
        out_shape=jax.ShapeDtypeStruct((bsz, seq, GLA_V), BF16),
        scratch_shapes=[pltpu.VMEM((GLA_HEADS, GLA_DV, GLA_DK), F32)],
        compiler_params=pltpu.CompilerParams(
            dimension_semantics=("arbitrary", "arbitrary"), vmem_limit_bytes=VMEM_LIMIT_BYTES),
        name="gla",
    )(z3, z3, z3, lr3, w_alpha, b_alpha, g_out)


def _sb_kernel(q_ref, k_ref, v_ref, gate_ref, o_ref, acc_ref, carry_ref):
    seq = q_ref.shape[0]
    tq, tk, n_streams = SB_Q_BLOCK, SB_K_BLOCK, SB_STREAMS
    scale = SB_DH ** -0.5
    r_id = lax.broadcasted_iota(jnp.int32, (tq, tk), 0)
    c_id = lax.broadcasted_iota(jnp.int32, (tq, tk), 1)
    strictly_before = c_id < r_id
    later_keys = jnp.where(r_id > c_id, 1.0, 0.0).astype(BF16)
    cum_and_total = jnp.concatenate([later_keys, jnp.ones((tk, tk), BF16)], axis=1)

    def step(i0, t, on_diagonal):
        log_betas, log_1ms, carries, vs = [], [], [], []
        for s in range(n_streams):
            j = i0 + s - t
            q0 = pl.multiple_of((i0 + s) * tq, tq)
            k0 = pl.multiple_of(jnp.maximum(j, 0) * tk, tk)
            sc = _dot_nt(q_ref[pl.ds(q0, tq), :], k_ref[pl.ds(k0, tk), :]) * scale
            log_1m = _log_sigmoid_neg(sc)
            log_betas.append(log_1m + sc)
            if on_diagonal:
                log_1m = jnp.where(strictly_before, log_1m, 0.0)
                carries.append(None)
            else:
                carries.append(jnp.where(j < 0, SB_NO_KEYS_LOG_WEIGHT, carry_ref[s]))
            log_1ms.append(log_1m)
            vs.append(v_ref[pl.ds(k0, tk), :])
        ct_all = _dot_f32_exact_rhs(jnp.concatenate(log_1ms, axis=0), cum_and_total)
        alive = None
        for s in range(n_streams):
            ct = ct_all[s * tq:(s + 1) * tq]
            after, total = ct[:, :tk], ct[:, tk:]
            if on_diagonal:
                w = jnp.where(strictly_before, jnp.exp(log_betas[s] + after), 0.0)
                acc_ref[s] = _dot(w.astype(BF16), vs[s])
                carry = total
            else:
                w = jnp.exp(log_betas[s] + (after + carries[s]))
                acc_ref[s] += _dot(w.astype(BF16), vs[s])
                carry = carries[s] + total
            carry_ref[s] = carry
            alive = carry if alive is None else jnp.maximum(alive, carry)
        return jnp.max(alive)

    def q_group(g, _):
        i0 = g * n_streams
        alive = step(i0, 0, True)

        def cond(st):
            return st[1] > SB_DEAD_LOG_WEIGHT

        def body(st):
            return st[0] + 1, step(i0, st[0], False)

        lax.while_loop(cond, body, (jnp.int32(1), alive))
        for s in range(n_streams):
            q0 = pl.multiple_of((i0 + s) * tq, tq)
            gate = gate_ref[pl.ds(q0, tq), :].astype(F32)
            o_ref[pl.ds(q0, tq), :] = (acc_ref[s] * _silu(gate)).astype(BF16)
        return 0

    lax.fori_loop(0, seq // (tq * n_streams), q_group, 0)


def _stick_breaking(z3):
    bsz, seq, _ = z3.shape

    def spec(off):
        return pl.BlockSpec((None, seq, SB_DH), lambda b, h: (b, 0, off // SB_DH + h))

    return pl.pallas_call(
        _sb_kernel,
        grid=(bsz, SB_HEADS),
        in_specs=[spec(OFF_Q_B), spec(OFF_K_B), spec(OFF_V_B), spec(OFF_GATE_B)],
        out_specs=pl.BlockSpec((None, seq, SB_DH), lambda b, h: (b, 0, h)),
        out_shape=jax.ShapeDtypeStruct((bsz, seq, SB_W), BF16),
        scratch_shapes=[pltpu.VMEM((SB_STREAMS, SB_Q_BLOCK, SB_DH), F32),
                        pltpu.VMEM((SB_STREAMS, SB_Q_BLOCK, SB_K_BLOCK), F32)],
        compiler_params=pltpu.CompilerParams(
            dimension_semantics=("arbitrary", "arbitrary"), vmem_limit_bytes=VMEM_LIMIT_BYTES),
        name="stickbrk",
    )(z3, z3, z3, z3)


def _out_proj_kernel(ga_ref, gb_ref, ma_ref, mb_ref, x_ref, p_ref, wa_ref, wb_ref, wo_ref, wg_ref, wp_ref,
                     gple_ref, gfin_ref, o_ref):
    y_a = _dot(ga_ref[...], wa_ref[...])
    y_b = _dot(gb_ref[...], wb_ref[...])
    merged = _sigmoid(ma_ref[...].astype(F32)) * y_a + _sigmoid(mb_ref[...].astype(F32)) * y_b
    x1 = x_ref[...] + _dot(merged.astype(BF16), wo_ref[...])
    u = (x1 * _rms_scale(x1) * gple_ref[...]).astype(BF16)
    gate_p = _sigmoid(_dot(u, wg_ref[...]))
    x2 = x1 + gate_p * _dot(p_ref[...].astype(BF16), wp_ref[...])
    o_ref[...] = x2 * _rms_scale(x2) * gfin_ref[...]


def _out_proj(ga, gb, z2, x2, p2, wa, wb, wo, wg, wp, g_ple, g_final):
    t = x2.shape[0]
    rows = OUT_PROJ_ROWS

    def row_spec(width, col_block=0):
        return pl.BlockSpec((rows, width), lambda i: (i, col_block))

    def full_spec(shape):
        return pl.BlockSpec(shape, lambda i: (0, 0))

    return pl.pallas_call(
        _out_proj_kernel,
        grid=(t // rows,),
        in_specs=[
            row_spec(GLA_V), row_spec(SB_W),
            row_spec(D_MODEL, OFF_M_A // D_MODEL), row_spec(D_MODEL, OFF_M_B // D_MODEL),
            row_spec(D_MODEL), row_spec(PLE_DIM),
            full_spec((GLA_V, D_MODEL)), full_spec((SB_W, D_MODEL)), full_spec((D_MODEL, D_MODEL)),
            full_spec((D_MODEL, D_MODEL)), full_spec((PLE_DIM, D_MODEL)),
            full_spec((1, D_MODEL)), full_spec((1, D_MODEL)),
        ],
        out_specs=row_spec(D_MODEL),
        out_shape=jax.ShapeDtypeStruct((t, D_MODEL), F32),
        compiler_params=pltpu.CompilerParams(
            dimension_semantics=("arbitrary",), vmem_limit_bytes=VMEM_LIMIT_BYTES),
        name="out_proj",
    )(ga, gb, z2, z2, x2, p2, wa, wb, wo, wg, wp, g_ple, g_final)


def _regroup_w_in(w):
    q_a, k_a, v_a, lr, gate_a, q_b, k_b, v_b, gate_b, m_a, m_b = jnp.split(
        w, [GLA_QK, 2 * GLA_QK, 2 * GLA_QK + GLA_V, 2 * GLA_QK + GLA_V + GLA_LOWRANK,
            2 * GLA_QK + 2 * GLA_V + GLA_LOWRANK,
            2 * GLA_QK + 2 * GLA_V + GLA_LOWRANK + SB_W,
            2 * GLA_QK + 2 * GLA_V + GLA_LOWRANK + 2 * SB_W,
            2 * GLA_QK + 2 * GLA_V + GLA_LOWRANK + 3 * SB_W,
            2 * GLA_QK + 2 * GLA_V + GLA_LOWRANK + 4 * SB_W,
            2 * GLA_QK + 2 * GLA_V + GLA_LOWRANK + 4 * SB_W + D_MODEL], axis=1)
    main = jnp.concatenate([q_a, k_a, v_a, gate_a, q_b, k_b, v_b, gate_b, m_a, m_b], axis=1).astype(BF16)
    lr = jnp.pad(lr, ((0, 0), (0, LANE - GLA_LOWRANK))).astype(BF16)
    return main, lr


def _layer(x, p, g_mix, w_in, w_alpha, b_alpha, g_gla_out, w_out_gla, w_out_sb, w_out, g_ple, w_ple_gate,
           w_ple, g_last):
    bsz, seq, _ = x.shape
    t = bsz * seq
    x2 = x.reshape(t, D_MODEL)
    w_main, w_lr = _regroup_w_in(w_in)
    z, lr = _in_proj(x2, g_mix.reshape(1, D_MODEL), w_main, w_lr)
    z3 = z.reshape(bsz, seq, Z_COLS)
    w_alpha_p = jnp.pad(w_alpha, ((0, LANE - GLA_LOWRANK), (0, 0)))
    ga = _gla(z3, lr.reshape(bsz, seq, LANE), w_alpha_p, b_alpha.reshape(1, GLA_QK),
              g_gla_out.reshape(1, GLA_V))
    gb = _stick_breaking(z3)
    out = _out_proj(ga.reshape(t, GLA_V), gb.reshape(t, SB_W), z, x2, p.reshape(t, PLE_DIM),
                    w_out_gla.astype(BF16), w_out_sb.astype(BF16), w_out.astype(BF16),
                    w_ple_gate.astype(BF16), w_ple.astype(BF16),
                    g_ple.reshape(1, D_MODEL), g_last.reshape(1, D_MODEL))
    return out.reshape(bsz, seq, D_MODEL)


def kernel(x, p, g_mix, w_in, w_alpha, b_alpha, g_gla_out, w_out_gla, w_out_sb, w_out, g_ple, w_ple_gate, w_ple,
           g_final):
    depth = p.shape[0]
    assert depth == 1, "the fused final RMSNorm assumes a single layer"
    return _layer(x, p[0], g_mix[0], w_in[0], w_alpha[0], b_alpha[0], g_gla_out[0], w_out_gla[0], w_out_sb[0],
                  w_out[0], g_ple[0], w_ple_gate[0], w_ple[0], g_final)
```

```python
import functools

import jax
import jax.numpy as jnp
from jax import lax
from jax.experimental import pallas as pl
from jax.experimental.pallas import tpu as pltpu

D_MODEL = 1024
CHUNK = 64
PLE_DIM = 256
GLA_HEADS = 4
GLA_DK = 128
GLA_DV = 256
GLA_LOWRANK = 16
GLA_TAU = 16.0
SB_HEADS = 8
SB_DH = 128
GLA_QK = GLA_HEADS * GLA_DK
GLA_V = GLA_HEADS * GLA_DV
SB_W = SB_HEADS * SB_DH
EPS = 1e-6

LANE = 128

OFF_QK_A = 0
OFF_V_A = OFF_QK_A + 2 * GLA_QK
OFF_GATE_A = OFF_V_A + GLA_V
OFF_Q_B = OFF_GATE_A + GLA_V
OFF_K_B = OFF_Q_B + SB_W
OFF_V_B = OFF_K_B + SB_W
OFF_GATE_B = OFF_V_B + SB_W
OFF_M_A = OFF_GATE_B + SB_W
OFF_M_B = OFF_M_A + D_MODEL
Z_COLS = OFF_M_B + D_MODEL

LOG2_E = 1.4426950408889634
SB_DEAD_DEFICIT = 104.0 * LOG2_E
SB_NO_KEYS_DEFICIT = 1e30
SB_STREAMS = 16

IN_PROJ_ROWS = 512
IN_PROJ_COLS = 1024
SB_Q_BLOCK = 128
SB_K_BLOCK = 128
OUT_ROWS = 512
GLA_ROWS = 256
GLA_ATTN_ROWS = 2 * CHUNK

VMEM_LIMIT_BYTES = 56 * 1024 * 1024

BF16 = jnp.bfloat16
F32 = jnp.float32


def _dot(a, b):
    return jnp.dot(a, b, preferred_element_type=F32)


def _dot_nt(a, b):
    return lax.dot_general(a, b, (((1,), (1,)), ((), ())), preferred_element_type=F32)


def _dot_tn(a, b):
    return lax.dot_general(a, b, (((0,), (0,)), ((), ())), preferred_element_type=F32)


def _split_hi_lo(a):
    hi = a.astype(BF16)
    lo = (a - hi.astype(F32)).astype(BF16)
    return hi, lo


def _dot_f32(a, b):
    a_hi, a_lo = _split_hi_lo(a)
    b_hi, b_lo = _split_hi_lo(b)
    return _dot(a_hi, b_hi) + (_dot(a_lo, b_hi) + _dot(a_hi, b_lo))


def _rms_scale(x):
    return lax.rsqrt(jnp.mean(x * x, axis=-1, keepdims=True) + EPS)


def _neg_abs(x):
    bits = lax.bitcast_convert_type(x, jnp.uint32) | jnp.uint32(0x80000000)
    return lax.bitcast_convert_type(bits, F32)


def _sigmoid(g):
    return 1.0 / (1.0 + jnp.exp(-g))


def _silu(g):
    return g * _sigmoid(g)


def _in_proj_kernel(x_ref, g_ref, w_ref, wlr_ref, z_ref, lr_ref):
    x = x_ref[...]
    h = (x * _rms_scale(x) * g_ref[...]).astype(BF16)
    for c in range(0, Z_COLS, IN_PROJ_COLS):
        z_ref[:, c:c + IN_PROJ_COLS] = _dot(h, w_ref[:, c:c + IN_PROJ_COLS]).astype(BF16)
    lr_ref[...] = _dot(h, wlr_ref[...])


def _in_proj(x2, g_mix, w_main, w_lr):
    t = x2.shape[0]
    return pl.pallas_call(
        _in_proj_kernel,
        grid=(t // IN_PROJ_ROWS,),
        in_specs=[
            pl.BlockSpec((IN_PROJ_ROWS, D_MODEL), lambda i: (i, 0)),
            pl.BlockSpec((1, D_MODEL), lambda i: (0, 0)),
            pl.BlockSpec((D_MODEL, Z_COLS), lambda i: (0, 0), pipeline_mode=pl.Buffered(1)),
            pl.BlockSpec((D_MODEL, LANE), lambda i: (0, 0)),
        ],
        out_specs=[
            pl.BlockSpec((IN_PROJ_ROWS, Z_COLS), lambda i: (i, 0)),
            pl.BlockSpec((IN_PROJ_ROWS, LANE), lambda i: (i, 0)),
        ],
        out_shape=[
            jax.ShapeDtypeStruct((t, Z_COLS), BF16),
            jax.ShapeDtypeStruct((t, LANE), F32),
        ],
        compiler_params=pltpu.CompilerParams(
            dimension_semantics=("arbitrary",), vmem_limit_bytes=VMEM_LIMIT_BYTES),
        name="in_proj",
    )(x2, g_mix, w_main, w_lr)


def _sb_kernel(q_ref, k_ref, v_ref, gate_ref, o_ref, acc_ref, spent_ref):
    seq = q_ref.shape[0]
    tq, tk, n_streams = SB_Q_BLOCK, SB_K_BLOCK, SB_STREAMS
    to_log2_logit = (SB_DH ** -0.5) * LOG2_E
    r_id = lax.broadcasted_iota(jnp.int32, (tq, tk), 0)
    c_id = lax.broadcasted_iota(jnp.int32, (tq, tk), 1)
    strictly_before = c_id < r_id
    later_keys = jnp.where(r_id > c_id, 1.0, 0.0).astype(BF16)

    def step(i0, t, on_diagonal, every_stream_has_keys=False):
        log2_betas, deficits, spents, vs = [], [], [], []
        for s in range(n_streams):
            j = i0 + s - t
            q0 = pl.multiple_of((i0 + s) * tq, tq)
            k0 = pl.multiple_of((j if every_stream_has_keys else jnp.maximum(j, 0)) * tk, tk)
            u = _dot_nt(q_ref[pl.ds(q0, tq), :], k_ref[pl.ds(k0, tk), :]) * to_log2_logit
            deficit = jnp.maximum(u, 0.0) + jnp.log2(1.0 + jnp.exp2(_neg_abs(u)))
            log2_betas.append(u - deficit)
            if on_diagonal:
                deficit = jnp.where(strictly_before, deficit, 0.0)
                spents.append(None)
            elif every_stream_has_keys:
                spents.append(spent_ref[s])
            else:
                spents.append(jnp.where(j < 0, SB_NO_KEYS_DEFICIT, spent_ref[s]))
            deficits.append(deficit)
            vs.append(v_ref[pl.ds(k0, tk), :])
        least_spent = None
        for s in range(n_streams):
            later = _dot(deficits[s].astype(BF16), later_keys)
            total = jnp.sum(deficits[s], axis=1, keepdims=True)
            if on_diagonal:
                w = jnp.where(strictly_before, jnp.exp2(log2_betas[s] - later), 0.0)
                acc_ref[s] = _dot(w.astype(BF16), vs[s])
                spent = jnp.broadcast_to(total, (tq, tk))
            else:
                w = jnp.exp2((log2_betas[s] - spents[s]) - later)
                acc_ref[s] += _dot(w.astype(BF16), vs[s])
                spent = spents[s] + total
            spent_ref[s] = spent
            least_spent = spent if least_spent is None else jnp.minimum(least_spent, spent)
        return jnp.min(least_spent)

    def q_group(g, _):
        i0 = g * n_streams
        least_spent = step(i0, 0, True)

        def alive(st):
            return st[1] < SB_DEAD_DEFICIT

        def alive_and_stream0_has_keys(st):
            return jnp.logical_and(alive(st), st[0] <= i0)

        def body_all_keys(st):
            return st[0] + 1, step(i0, st[0], False, every_stream_has_keys=True)

        def body(st):
            return st[0] + 1, step(i0, st[0], False)

        st = lax.while_loop(alive_and_stream0_has_keys, body_all_keys, (jnp.int32(1), least_spent))
        lax.while_loop(alive, body, st)
        for s in range(n_streams):
            q0 = pl.multiple_of((i0 + s) * tq, tq)
            gate = gate_ref[pl.ds(q0, tq), :].astype(F32)
            o_ref[pl.ds(q0, tq), :] = (acc_ref[s] * _silu(gate)).astype(BF16)
        return 0

    lax.fori_loop(0, seq // (tq * n_streams), q_group, 0)


def _stick_breaking(z3):
    bsz, seq, _ = z3.shape

    def spec(off):
        return pl.BlockSpec((None, seq, SB_DH), lambda b, h: (b, 0, off // SB_DH + h))

    return pl.pallas_call(
        _sb_kernel,
        grid=(bsz, SB_HEADS),
        in_specs=[spec(OFF_Q_B), spec(OFF_K_B), spec(OFF_V_B), spec(OFF_GATE_B)],
        out_specs=pl.BlockSpec((None, seq, SB_DH), lambda b, h: (b, 0, h)),
        out_shape=jax.ShapeDtypeStruct((bsz, seq, SB_W), BF16),
        scratch_shapes=[pltpu.VMEM((SB_STREAMS, SB_Q_BLOCK, SB_DH), F32),
                        pltpu.VMEM((SB_STREAMS, SB_Q_BLOCK, SB_K_BLOCK), F32)],
        compiler_params=pltpu.CompilerParams(
            dimension_semantics=("arbitrary", "arbitrary"), vmem_limit_bytes=VMEM_LIMIT_BYTES),
        name="stickbrk",
    )(z3, z3, z3, z3)


def _gla_rows(r0, qk_ref, v_ref, gate_ref, lr_ref, walpha_ref, balpha_ref, gout_ref, state_ref, ga_ref):
    rows, arows = GLA_ROWS, GLA_ATTN_ROWS
    rs = slice(r0, r0 + rows)
    chunk_shift = CHUNK.bit_length() - 1

    r_id = lax.broadcasted_iota(jnp.int32, (rows, rows), 0)
    c_id = lax.broadcasted_iota(jnp.int32, (rows, rows), 1)
    same_chunk = jnp.right_shift(r_id, chunk_shift) == jnp.right_shift(c_id, chunk_shift)
    cum_and_rest = jnp.concatenate([jnp.where(same_chunk & (r_id >= c_id), 1.0, 0.0),
                                    jnp.where(same_chunk & (r_id < c_id), 1.0, 0.0)], axis=0).astype(BF16)
    ar_id = lax.broadcasted_iota(jnp.int32, (arows, arows), 0)
    ac_id = lax.broadcasted_iota(jnp.int32, (arows, arows), 1)
    attn_same_chunk = jnp.right_shift(ar_id, chunk_shift) == jnp.right_shift(ac_id, chunk_shift)
    attn_causal = ar_id >= ac_id

    logits = _dot_f32(lr_ref[rs, :], walpha_ref[...]) + balpha_ref[...]
    log_a = (jnp.minimum(logits, 0.0) - jnp.log(1.0 + jnp.exp(_neg_abs(logits)))) * (1.0 / GLA_TAU)
    la_hi, la_lo = _split_hi_lo(log_a)
    b_and_rest = _dot(cum_and_rest, la_hi) + _dot(cum_and_rest, la_lo)

    for h in range(GLA_HEADS):
        ks = slice(h * GLA_DK, (h + 1) * GLA_DK)
        vs = slice(h * GLA_DV, (h + 1) * GLA_DV)
        q = qk_ref[rs, h * GLA_DK:(h + 1) * GLA_DK]
        k = qk_ref[rs, GLA_QK + h * GLA_DK:GLA_QK + (h + 1) * GLA_DK]
        v = v_ref[rs, vs]
        b = b_and_rest[:rows, ks]
        rest = b_and_rest[rows:, ks]
        eb = jnp.exp(b)
        enb = jnp.exp(-b)
        q_scale = GLA_DK ** -0.5
        qd = q * (eb * q_scale).astype(BF16)
        kd = k * enb.astype(BF16)
        qi = q * (enb * q_scale).astype(BF16)
        ki = k * eb.astype(BF16)
        k_end = k * jnp.exp(rest).astype(BF16)

        o_intra = []
        for a0 in range(0, rows, arows):
            sl = slice(a0, a0 + arows)
            a_past = _dot_nt(qd[sl], kd[sl])
            a_fut = _dot_nt(qi[sl], ki[sl])
            attn = jnp.where(attn_same_chunk, jnp.where(attn_causal, a_past, a_fut), 0.0)
            o_intra.append(_dot(attn.astype(BF16), v[sl]))

        state = state_ref[h]
        o_inter = []
        for c0 in range(0, rows, CHUNK):
            sl = slice(c0, c0 + CHUNK)
            o_inter.append(_dot_nt(qd[sl], state.astype(BF16)))
            decay = jnp.exp(b[c0:c0 + 1] + rest[c0:c0 + 1])
            state = state * decay + _dot_tn(v[sl], k_end[sl])
        state_ref[h] = state

        o = jnp.concatenate(o_intra, axis=0) + jnp.concatenate(o_inter, axis=0)
        o = o * _rms_scale(o) * gout_ref[:, vs]
        ga_ref[rs, vs] = (o * _silu(gate_ref[rs, vs].astype(F32))).astype(BF16)


def _gla_out_kernel(qk_ref, va_ref, gatea_ref, lr_ref, walpha_ref, balpha_ref, gout_ref,
                    gb_ref, ma_ref, mb_ref, x_ref, p_ref, wa_ref, wb_ref, wo_ref, wg_ref, wp_ref,
                    gple_ref, gfin_ref, o_ref, state_ref, ga_ref, *, steps_per_seq):
    @pl.when(pl.program_id(0) % steps_per_seq == 0)
    def _():
        state_ref[...] = jnp.zeros_like(state_ref)

    for r0 in range(0, OUT_ROWS, GLA_ROWS):
        _gla_rows(r0, qk_ref, va_ref, gatea_ref, lr_ref, walpha_ref, balpha_ref, gout_ref, state_ref, ga_ref)

    y_a = _dot(ga_ref[...], wa_ref[...])
    y_b = _dot(gb_ref[...], wb_ref[...])
    merged = _sigmoid(ma_ref[...].astype(F32)) * y_a + _sigmoid(mb_ref[...].astype(F32)) * y_b
    x1 = x_ref[...] + _dot(merged.astype(BF16), wo_ref[...])
    u = (x1 * _rms_scale(x1) * gple_ref[...]).astype(BF16)
    gate_p = _sigmoid(_dot(u, wg_ref[...]))
    x2 = x1 + gate_p * _dot(p_ref[...].astype(BF16), wp_ref[...])
    o_ref[...] = x2 * _rms_scale(x2) * gfin_ref[...]


def _gla_out(z2, lr2, gb, x2, p2, seq, w_alpha, b_alpha, g_out, wa, wb, wo, wg, wp, g_ple, g_final):
    t = x2.shape[0]
    rows = OUT_ROWS
    assert seq % rows == 0

    def row_spec(width, col_block=0):
        return pl.BlockSpec((rows, width), lambda i: (i, col_block))

    def full_spec(shape):
        return pl.BlockSpec(shape, lambda i: (0, 0), pipeline_mode=pl.Buffered(1))

    return pl.pallas_call(
        functools.partial(_gla_out_kernel, steps_per_seq=seq // rows),
        grid=(t // rows,),
        in_specs=[
            row_spec(2 * GLA_QK, OFF_QK_A // D_MODEL), row_spec(GLA_V, OFF_V_A // D_MODEL),
            row_spec(GLA_V, OFF_GATE_A // D_MODEL), row_spec(LANE),
            full_spec((LANE, GLA_QK)), full_spec((1, GLA_QK)), full_spec((1, GLA_V)),
            row_spec(SB_W),
            row_spec(D_MODEL, OFF_M_A // D_MODEL), row_spec(D_MODEL, OFF_M_B // D_MODEL),
            row_spec(D_MODEL), row_spec(PLE_DIM),
            full_spec((GLA_V, D_MODEL)), full_spec((SB_W, D_MODEL)), full_spec((D_MODEL, D_MODEL)),
            full_spec((D_MODEL, D_MODEL)), full_spec((PLE_DIM, D_MODEL)),
            full_spec((1, D_MODEL)), full_spec((1, D_MODEL)),
        ],
        out_specs=row_spec(D_MODEL),
        out_shape=jax.ShapeDtypeStruct((t, D_MODEL), F32),
        scratch_shapes=[pltpu.VMEM((GLA_HEADS, GLA_DV, GLA_DK), F32),
                        pltpu.VMEM((rows, GLA_V), BF16)],
        compiler_params=pltpu.CompilerParams(
            dimension_semantics=("arbitrary",), vmem_limit_bytes=VMEM_LIMIT_BYTES),
        name="gla_out",
    )(z2, z2, z2, lr2, w_alpha, b_alpha, g_out, gb, z2, z2, x2, p2, wa, wb, wo, wg, wp, g_ple, g_final)


def _regroup_w_in(w):
    q_a, k_a, v_a, lr, gate_a, q_b, k_b, v_b, gate_b, m_a, m_b = jnp.split(
        w, [GLA_QK, 2 * GLA_QK, 2 * GLA_QK + GLA_V, 2 * GLA_QK + GLA_V + GLA_LOWRANK,
            2 * GLA_QK + 2 * GLA_V + GLA_LOWRANK,
            2 * GLA_QK + 2 * GLA_V + GLA_LOWRANK + SB_W,
            2 * GLA_QK + 2 * GLA_V + GLA_LOWRANK + 2 * SB_W,
            2 * GLA_QK + 2 * GLA_V + GLA_LOWRANK + 3 * SB_W,
            2 * GLA_QK + 2 * GLA_V + GLA_LOWRANK + 4 * SB_W,
            2 * GLA_QK + 2 * GLA_V + GLA_LOWRANK + 4 * SB_W + D_MODEL], axis=1)
    main = jnp.concatenate([q_a, k_a, v_a, gate_a, q_b, k_b, v_b, gate_b, m_a, m_b], axis=1).astype(BF16)
    lr = jnp.pad(lr, ((0, 0), (0, LANE - GLA_LOWRANK))).astype(BF16)
    return main, lr


def _layer(x, p, g_mix, w_in, w_alpha, b_alpha, g_gla_out, w_out_gla, w_out_sb, w_out, g_ple, w_ple_gate,
           w_ple, g_last):
    bsz, seq, _ = x.shape
    t = bsz * seq
    x2 = x.reshape(t, D_MODEL)
    w_main, w_lr = _regroup_w_in(w_in)
    z, lr = _in_proj(x2, g_mix.reshape(1, D_MODEL), w_main, w_lr)
    gb = _stick_breaking(z.reshape(bsz, seq, Z_COLS))
    w_alpha_p = jnp.pad(w_alpha, ((0, LANE - GLA_LOWRANK), (0, 0)))
    out = _gla_out(z, lr, gb.reshape(t, SB_W), x2, p.reshape(t, PLE_DIM), seq,
                   w_alpha_p, b_alpha.reshape(1, GLA_QK), g_gla_out.reshape(1, GLA_V),
                   w_out_gla.astype(BF16), w_out_sb.astype(BF16), w_out.astype(BF16),
                   w_ple_gate.astype(BF16), w_ple.astype(BF16),
                   g_ple.reshape(1, D_MODEL), g_last.reshape(1, D_MODEL))
    return out.reshape(bsz, seq, D_MODEL)


def kernel(x, p, g_mix, w_in, w_alpha, b_alpha, g_gla_out, w_out_gla, w_out_sb, w_out, g_ple, w_ple_gate, w_ple,
           g_final):
    depth = p.shape[0]
    assert depth == 1, "the fused final RMSNorm assumes a single layer"
    return _layer(x, p[0], g_mix[0], w_in[0], w_alpha[0], b_alpha[0], g_gla_out[0], w_out_gla[0], w_out_sb[0],
                  w_out[0], g_ple[0], w_ple_gate[0], w_ple[0], g_final)
```

```python
import jax
import jax.numpy as jnp
from jax import lax
from jax.experimental import pallas as pl
from jax.experimental.pallas import tpu as pltpu

D_MODEL = 1024
CHUNK = 64
PLE_DIM = 256
GLA_HEADS = 4
GLA_DK = 128
GLA_DV = 256
GLA_LOWRANK = 16
GLA_TAU = 16.0
SB_HEADS = 8
SB_DH = 128
GLA_QK = GLA_HEADS * GLA_DK
GLA_V = GLA_HEADS * GLA_DV
SB_W = SB_HEADS * SB_DH
EPS = 1e-6

LANE = 128

OFF_QK_A = 0
OFF_V_A = OFF_QK_A + 2 * GLA_QK
OFF_GATE_A = OFF_V_A + GLA_V
OFF_Q_B = OFF_GATE_A + GLA_V
OFF_K_B = OFF_Q_B + SB_W
OFF_V_B = OFF_K_B + SB_W
OFF_GATE_B = OFF_V_B + SB_W
OFF_M_A = OFF_GATE_B + SB_W
OFF_M_B = OFF_M_A + D_MODEL
Z_COLS = OFF_M_B + D_MODEL

LOG2_E = 1.4426950408889634
SB_DEAD_DEFICIT = 104.0 * LOG2_E
SB_NO_KEYS_DEFICIT = 1e30
SB_STREAMS = 16

IN_PROJ_ROWS = 512
IN_PROJ_COLS = 1024
SB_Q_BLOCK = 128
SB_K_BLOCK = 128
OUT_PROJ_ROWS = 512
GLA_ROWS = 256
GLA_ATTN_ROWS = 2 * CHUNK

VMEM_LIMIT_BYTES = 56 * 1024 * 1024

BF16 = jnp.bfloat16
F32 = jnp.float32


def _dot(a, b):
    return jnp.dot(a, b, preferred_element_type=F32)


def _dot_nt(a, b):
    return lax.dot_general(a, b, (((1,), (1,)), ((), ())), preferred_element_type=F32)


def _dot_tn(a, b):
    return lax.dot_general(a, b, (((0,), (0,)), ((), ())), preferred_element_type=F32)


def _rms_scale(x):
    return lax.rsqrt(jnp.mean(x * x, axis=-1, keepdims=True) + EPS)


def _sigmoid(g):
    return 1.0 / (1.0 + jnp.exp(-g))


def _silu(g):
    return g * _sigmoid(g)


def _in_proj_kernel(x_ref, g_ref, w_head_ref, w_tail_ref, wlr_ref, z_ref, lr_ref):
    x = x_ref[...]
    h = (x * _rms_scale(x) * g_ref[...]).astype(BF16)
    for c in range(0, OFF_GATE_A, IN_PROJ_COLS):
        z_ref[:, c:c + IN_PROJ_COLS] = _dot(h, w_head_ref[:, c:c + IN_PROJ_COLS]).astype(BF16)
    for c in range(0, Z_COLS - OFF_GATE_A, IN_PROJ_COLS):
        z_ref[:, OFF_GATE_A + c:OFF_GATE_A + c + IN_PROJ_COLS] = _dot(
            h, w_tail_ref[:, c:c + IN_PROJ_COLS]).astype(BF16)
    lr_ref[...] = _dot(h, wlr_ref[...])


def _in_proj(x2, g_mix, w_head, w_tail, w_lr):
    t = x2.shape[0]
    return pl.pallas_call(
        _in_proj_kernel,
        grid=(t // IN_PROJ_ROWS,),
        in_specs=[
            pl.BlockSpec((IN_PROJ_ROWS, D_MODEL), lambda i: (i, 0)),
            pl.BlockSpec((1, D_MODEL), lambda i: (0, 0)),
            pl.BlockSpec((D_MODEL, OFF_GATE_A), lambda i: (0, 0), pipeline_mode=pl.Buffered(1)),
            pl.BlockSpec((D_MODEL, Z_COLS - OFF_GATE_A), lambda i: (0, 0), pipeline_mode=pl.Buffered(1)),
            pl.BlockSpec((D_MODEL, LANE), lambda i: (0, 0)),
        ],
        out_specs=[
            pl.BlockSpec((IN_PROJ_ROWS, Z_COLS), lambda i: (i, 0)),
            pl.BlockSpec((IN_PROJ_ROWS, LANE), lambda i: (i, 0)),
        ],
        out_shape=[
            jax.ShapeDtypeStruct((t, Z_COLS), BF16),
            jax.ShapeDtypeStruct((t, LANE), F32),
        ],
        compiler_params=pltpu.CompilerParams(
            dimension_semantics=("arbitrary",), vmem_limit_bytes=VMEM_LIMIT_BYTES),
        name="in_proj",
    )(x2, g_mix, w_head, w_tail, w_lr)


def _gla_kernel(qk_ref, v_ref, gate_ref, lr_ref, walpha_ref, balpha_ref, gout_ref, o_ref, state_ref):
    rows, arows = GLA_ROWS, GLA_ATTN_ROWS
    chunk_shift = CHUNK.bit_length() - 1

    @pl.when(pl.program_id(1) == 0)
    def _():
        state_ref[...] = jnp.zeros_like(state_ref)

    r_id = lax.broadcasted_iota(jnp.int32, (rows, rows), 0)
    c_id = lax.broadcasted_iota(jnp.int32, (rows, rows), 1)
    same_chunk = jnp.right_shift(r_id, chunk_shift) == jnp.right_shift(c_id, chunk_shift)
    cum_and_rest = jnp.concatenate([jnp.where(same_chunk & (r_id >= c_id), 1.0, 0.0),
                                    jnp.where(same_chunk & (r_id < c_id), 1.0, 0.0)], axis=0).astype(BF16)
    ar_id = lax.broadcasted_iota(jnp.int32, (arows, arows), 0)
    ac_id = lax.broadcasted_iota(jnp.int32, (arows, arows), 1)
    attn_same_chunk = jnp.right_shift(ar_id, chunk_shift) == jnp.right_shift(ac_id, chunk_shift)
    attn_causal = ar_id >= ac_id

    logits = _dot(lr_ref[...].astype(BF16), walpha_ref[...]) + balpha_ref[...]
    log_a = (jnp.minimum(logits, 0.0) - jnp.log(1.0 + jnp.exp(-jnp.abs(logits)))) * (1.0 / GLA_TAU)
    b_and_rest = _dot(cum_and_rest, log_a.astype(BF16))

    q_scale = GLA_DK ** -0.5
    qd, kd, qi, ki, k_end, v, decays, o_intra, o_inter = {}, {}, {}, {}, {}, {}, {}, {}, {}

    def decay_products(h):
        ks = slice(h * GLA_DK, (h + 1) * GLA_DK)
        q = qk_ref[:, h * GLA_DK:(h + 1) * GLA_DK]
        k = qk_ref[:, GLA_QK + h * GLA_DK:GLA_QK + (h + 1) * GLA_DK]
        b = b_and_rest[:rows, ks]
        rest = b_and_rest[rows:, ks]
        eb = jnp.exp(b)
        enb = jnp.exp(-b)
        qd[h] = q * (eb * q_scale).astype(BF16)
        kd[h] = k * enb.astype(BF16)
        qi[h] = q * (enb * q_scale).astype(BF16)
        ki[h] = k * eb.astype(BF16)
        k_end[h] = k * jnp.exp(rest).astype(BF16)
        v[h] = v_ref[:, h * GLA_DV:(h + 1) * GLA_DV]
        decays[h] = [jnp.exp(b[c0:c0 + 1] + rest[c0:c0 + 1]) for c0 in range(0, rows, CHUNK)]

    def within_chunks(h):
        tiles = []
        for a0 in range(0, rows, arows):
            sl = slice(a0, a0 + arows)
            a_past = _dot_nt(qd[h][sl], kd[h][sl])
            a_fut = _dot_nt(qi[h][sl], ki[h][sl])
            attn = jnp.where(attn_same_chunk, jnp.where(attn_causal, a_past, a_fut), 0.0)
            tiles.append(_dot(attn.astype(BF16), v[h][sl]))
        o_intra[h] = jnp.concatenate(tiles, axis=0)

    def across_chunks(h):
        state = state_ref[h]
        tiles = []
        for c, c0 in enumerate(range(0, rows, CHUNK)):
            sl = slice(c0, c0 + CHUNK)
            tiles.append(_dot_nt(qd[h][sl], state.astype(BF16)))
            state = state * decays[h][c] + _dot_tn(v[h][sl], k_end[h][sl])
        state_ref[h] = state
        o_inter[h] = jnp.concatenate(tiles, axis=0)

    def normalise_and_gate(h):
        vs = slice(h * GLA_DV, (h + 1) * GLA_DV)
        o = o_intra[h] + o_inter[h]
        o = o * _rms_scale(o) * gout_ref[:, vs]
        o_ref[:, vs] = (o * _silu(gate_ref[:, vs].astype(F32))).astype(BF16)

    for h in range(GLA_HEADS):
        decay_products(h)
        within_chunks(h)
        across_chunks(h)
        normalise_and_gate(h)


def _gla(z3, lr3, w_alpha, b_alpha, g_out):
    bsz, seq, _ = z3.shape
    nblk = D_MODEL
    return pl.pallas_call(
        _gla_kernel,
        grid=(bsz, seq // GLA_ROWS),
        in_specs=[
            pl.BlockSpec((None, GLA_ROWS, 2 * GLA_QK), lambda b, s: (b, s, OFF_QK_A // nblk)),
            pl.BlockSpec((None, GLA_ROWS, GLA_V), lambda b, s: (b, s, OFF_V_A // nblk)),
            pl.BlockSpec((None, GLA_ROWS, GLA_V), lambda b, s: (b, s, OFF_GATE_A // nblk)),
            pl.BlockSpec((None, GLA_ROWS, LANE), lambda b, s: (b, s, 0)),
            pl.BlockSpec((LANE, GLA_QK), lambda b, s: (0, 0)),
            pl.BlockSpec((1, GLA_QK), lambda b, s: (0, 0)),
            pl.BlockSpec((1, GLA_V), lambda b, s: (0, 0)),
        ],
        out_specs=pl.BlockSpec((None, GLA_ROWS, GLA_V), lambda b, s: (b, s, 0)),
        out_shape=jax.ShapeDtypeStruct((bsz, seq, GLA_V), BF16),
        scratch_shapes=[pltpu.VMEM((GLA_HEADS, GLA_DV, GLA_DK), F32)],
        compiler_params=pltpu.CompilerParams(
            dimension_semantics=("arbitrary", "arbitrary"), vmem_limit_bytes=VMEM_LIMIT_BYTES),
        name="gla",
    )(z3, z3, z3, lr3, w_alpha, b_alpha, g_out)


def _sb_kernel(q_ref, k_ref, v_ref, gate_ref, o_ref, acc_ref, spent_ref):
    seq = q_ref.shape[0]
    tq, tk, n_streams = SB_Q_BLOCK, SB_K_BLOCK, SB_STREAMS
    to_log2_logit = (SB_DH ** -0.5) * LOG2_E
    r_id = lax.broadcasted_iota(jnp.int32, (tq, tk), 0)
    c_id = lax.broadcasted_iota(jnp.int32, (tq, tk), 1)
    strictly_before = c_id < r_id
    later_keys = jnp.where(r_id > c_id, 1.0, 0.0).astype(BF16)

    def step(i0, t, on_diagonal, every_stream_has_keys=False):
        log2_betas, deficits, spents, vs = [], [], [], []
        for s in range(n_streams):
            j = i0 + s - t
            q0 = pl.multiple_of((i0 + s) * tq, tq)
            k0 = pl.multiple_of((j if every_stream_has_keys else jnp.maximum(j, 0)) * tk, tk)
            u = _dot_nt(q_ref[pl.ds(q0, tq), :], k_ref[pl.ds(k0, tk), :]) * to_log2_logit
            deficit = jnp.maximum(u, 0.0) + jnp.log2(1.0 + 1.0 / jnp.exp2(jnp.abs(u)))
            log2_betas.append(u - deficit)
            if on_diagonal:
                deficit = jnp.where(strictly_before, deficit, 0.0)
                spents.append(None)
            elif every_stream_has_keys:
                spents.append(spent_ref[s])
            else:
                spents.append(jnp.where(j < 0, SB_NO_KEYS_DEFICIT, spent_ref[s]))
            deficits.append(deficit)
            vs.append(v_ref[pl.ds(k0, tk), :])
        least_spent = None
        for s in range(n_streams):
            later = _dot(deficits[s].astype(BF16), later_keys)
            total = jnp.sum(deficits[s], axis=1, keepdims=True)
            if on_diagonal:
                w = jnp.where(strictly_before, jnp.exp2(log2_betas[s] - later), 0.0)
                acc_ref[s] = _dot(w.astype(BF16), vs[s])
                spent = jnp.broadcast_to(total, (tq, tk))
            else:
                w = jnp.exp2((log2_betas[s] - spents[s]) - later)
                acc_ref[s] += _dot(w.astype(BF16), vs[s])
                spent = spents[s] + total
            spent_ref[s] = spent
            least_spent = spent if least_spent is None else jnp.minimum(least_spent, spent)
        return jnp.min(least_spent)

    def q_group(g, _):
        i0 = g * n_streams
        least_spent = step(i0, 0, True)

        def alive(st):
            return st[1] < SB_DEAD_DEFICIT

        def alive_and_stream0_has_keys(st):
            return jnp.logical_and(alive(st), st[0] <= i0)

        def body_all_keys(st):
            return st[0] + 1, step(i0, st[0], False, every_stream_has_keys=True)

        def body(st):
            return st[0] + 1, step(i0, st[0], False)

        st = lax.while_loop(alive_and_stream0_has_keys, body_all_keys, (jnp.int32(1), least_spent))
        lax.while_loop(alive, body, st)
        for s in range(n_streams):
            q0 = pl.multiple_of((i0 + s) * tq, tq)
            gate = gate_ref[pl.ds(q0, tq), :].astype(F32)
            o_ref[pl.ds(q0, tq), :] = (acc_ref[s] * _silu(gate)).astype(BF16)
        return 0

    lax.fori_loop(0, seq // (tq * n_streams), q_group, 0)


def _stick_breaking(z3):
    bsz, seq, _ = z3.shape

    def spec(off):
        return pl.BlockSpec((None, seq, SB_DH), lambda b, h: (b, 0, off // SB_DH + h))

    return pl.pallas_call(
        _sb_kernel,
        grid=(bsz, SB_HEADS),
        in_specs=[spec(OFF_Q_B), spec(OFF_K_B), spec(OFF_V_B), spec(OFF_GATE_B)],
        out_specs=pl.BlockSpec((None, seq, SB_DH), lambda b, h: (b, 0, h)),
        out_shape=jax.ShapeDtypeStruct((bsz, seq, SB_W), BF16),
        scratch_shapes=[pltpu.VMEM((SB_STREAMS, SB_Q_BLOCK, SB_DH), F32),
                        pltpu.VMEM((SB_STREAMS, SB_Q_BLOCK, SB_K_BLOCK), F32)],
        compiler_params=pltpu.CompilerParams(
            dimension_semantics=("arbitrary", "arbitrary"), vmem_limit_bytes=VMEM_LIMIT_BYTES),
        name="stickbrk",
    )(z3, z3, z3, z3)


def _out_proj_kernel(ga_ref, gb_ref, ma_ref, mb_ref, x_ref, p_ref, wa_ref, wb_ref, wo_ref, wg_ref, wp_ref,
                     gple_ref, gfin_ref, o_ref):
    y_a = _dot(ga_ref[...], wa_ref[...])
    y_b = _dot(gb_ref[...], wb_ref[...])
    merged = _sigmoid(ma_ref[...].astype(F32)) * y_a + _sigmoid(mb_ref[...].astype(F32)) * y_b
    x1 = x_ref[...] + _dot(merged.astype(BF16), wo_ref[...])
    u = (x1 * _rms_scale(x1) * gple_ref[...]).astype(BF16)
    gate_p = _sigmoid(_dot(u, wg_ref[...]))
    x2 = x1 + gate_p * _dot(p_ref[...].astype(BF16), wp_ref[...])
    o_ref[...] = x2 * _rms_scale(x2) * gfin_ref[...]


def _out_proj(ga, gb, z2, x2, p2, wa, wb, wo, wg, wp, g_ple, g_final):
    t = x2.shape[0]
    rows = OUT_PROJ_ROWS

    def row_spec(width, col_block=0):
        return pl.BlockSpec((rows, width), lambda i: (i, col_block))

    def full_spec(shape):
        return pl.BlockSpec(shape, lambda i: (0, 0), pipeline_mode=pl.Buffered(1))

    return pl.pallas_call(
        _out_proj_kernel,
        grid=(t // rows,),
        in_specs=[
            row_spec(GLA_V), row_spec(SB_W),
            row_spec(D_MODEL, OFF_M_A // D_MODEL), row_spec(D_MODEL, OFF_M_B // D_MODEL),
            row_spec(D_MODEL), row_spec(PLE_DIM),
            full_spec((GLA_V, D_MODEL)), full_spec((SB_W, D_MODEL)), full_spec((D_MODEL, D_MODEL)),
            full_spec((D_MODEL, D_MODEL)), full_spec((PLE_DIM, D_MODEL)),
            full_spec((1, D_MODEL)), full_spec((1, D_MODEL)),
        ],
        out_specs=row_spec(D_MODEL),
        out_shape=jax.ShapeDtypeStruct((t, D_MODEL), F32),
        compiler_params=pltpu.CompilerParams(
            dimension_semantics=("arbitrary",), vmem_limit_bytes=VMEM_LIMIT_BYTES),
        name="out_proj",
    )(ga, gb, z2, z2, x2, p2, wa, wb, wo, wg, wp, g_ple, g_final)


def _split_w_in(w):
    lr0 = OFF_GATE_A
    lr1 = lr0 + GLA_LOWRANK
    lr = jnp.pad(w[:, lr0:lr1], ((0, 0), (0, LANE - GLA_LOWRANK)))
    return w[:, :lr0].astype(BF16), w[:, lr1:].astype(BF16), lr.astype(BF16)


def _layer(x, p, g_mix, w_in, w_alpha, b_alpha, g_gla_out, w_out_gla, w_out_sb, w_out, g_ple, w_ple_gate,
           w_ple, g_last):
    bsz, seq, _ = x.shape
    t = bsz * seq
    x2 = x.reshape(t, D_MODEL)
    z, lr = _in_proj(x2, g_mix.reshape(1, D_MODEL), *_split_w_in(w_in))
    z3 = z.reshape(bsz, seq, Z_COLS)
    w_alpha_p = jnp.pad(w_alpha, ((0, LANE - GLA_LOWRANK), (0, 0))).astype(BF16)
    ga = _gla(z3, lr.reshape(bsz, seq, LANE), w_alpha_p, b_alpha.reshape(1, GLA_QK),
              g_gla_out.reshape(1, GLA_V))
    gb = _stick_breaking(z3)
    out = _out_proj(ga.reshape(t, GLA_V), gb.reshape(t, SB_W), z, x2, p.reshape(t, PLE_DIM),
                    w_out_gla.astype(BF16), w_out_sb.astype(BF16), w_out.astype(BF16),
                    w_ple_gate.astype(BF16), w_ple.astype(BF16),
                    g_ple.reshape(1, D_MODEL), g_last.reshape(1, D_MODEL))
    return out.reshape(bsz, seq, D_MODEL)


def kernel(x, p, g_mix, w_in, w_alpha, b_alpha, g_gla_out, w_out_gla, w_out_sb, w_out, g_ple, w_ple_gate, w_ple,
           g_final):
    depth = p.shape[0]
    assert depth == 1, "the fused final RMSNorm assumes a single layer"
    return _layer(x, p[0], g_mix[0], w_in[0], w_alpha[0], b_alpha[0], g_gla_out[0], w_out_gla[0], w_out_sb[0],
                  w_out[0], g_ple[0], w_ple_gate[0], w_ple[0], g_final)
```

```python
import jax
import jax.numpy as jnp
from jax import lax
from jax.experimental import pallas as pl
from jax.experimental.pallas import tpu as pltpu

D_MODEL = 1024
CHUNK = 64
PLE_DIM = 256
GLA_HEADS = 4
GLA_DK = 128
GLA_DV = 256
GLA_LOWRANK = 16
GLA_TAU = 16.0
SB_HEADS = 8
SB_DH = 128
GLA_QK = GLA_HEADS * GLA_DK
GLA_V = GLA_HEADS * GLA_DV
SB_W = SB_HEADS * SB_DH
EPS = 1e-6

LANE = 128

OFF_QK_A = 0
OFF_V_A = OFF_QK_A + 2 * GLA_QK
OFF_GATE_A = OFF_V_A + GLA_V
OFF_Q_B = OFF_GATE_A + GLA_V
OFF_K_B = OFF_Q_B + SB_W
OFF_V_B = OFF_K_B + SB_W
OFF_GATE_B = OFF_V_B + SB_W
OFF_M_A = OFF_GATE_B + SB_W
OFF_M_B = OFF_M_A + D_MODEL
Z_COLS = OFF_M_B + D_MODEL

LOG2_E = 1.4426950408889634
SB_DEAD_DEFICIT = 104.0 * LOG2_E
SB_NO_KEYS_DEFICIT = 1e30
SB_STREAMS = 32

IN_PROJ_ROWS = 512
IN_PROJ_COLS = 1024
SB_Q_BLOCK = 128
SB_K_BLOCK = 128
OUT_PROJ_ROWS = 1024
GLA_ROWS = 256
GLA_ATTN_ROWS = 2 * CHUNK

VMEM_LIMIT_BYTES = 56 * 1024 * 1024

BF16 = jnp.bfloat16
F32 = jnp.float32


def _dot(a, b):
    return jnp.dot(a, b, preferred_element_type=F32)


def _dot_nt(a, b):
    return lax.dot_general(a, b, (((1,), (1,)), ((), ())), preferred_element_type=F32)


def _dot_tn(a, b):
    return lax.dot_general(a, b, (((0,), (0,)), ((), ())), preferred_element_type=F32)


def _rms_scale(x):
    return lax.rsqrt(jnp.mean(x * x, axis=-1, keepdims=True) + EPS)


def _sigmoid(g):
    return 1.0 / (1.0 + jnp.exp(-g))


def _silu(g):
    return g * _sigmoid(g)


def _in_proj_kernel(x_ref, g_ref, w_head_ref, w_tail_ref, wlr_ref, z_ref, lr_ref):
    x = x_ref[...]
    h = (x * _rms_scale(x) * g_ref[...]).astype(BF16)
    for c in range(0, OFF_GATE_A, IN_PROJ_COLS):
        z_ref[:, c:c + IN_PROJ_COLS] = _dot(h, w_head_ref[:, c:c + IN_PROJ_COLS]).astype(BF16)
    for c in range(0, Z_COLS - OFF_GATE_A, IN_PROJ_COLS):
        z_ref[:, OFF_GATE_A + c:OFF_GATE_A + c + IN_PROJ_COLS] = _dot(
            h, w_tail_ref[:, c:c + IN_PROJ_COLS]).astype(BF16)
    lr_ref[...] = _dot(h, wlr_ref[...])


def _in_proj(x2, g_mix, w_head, w_tail, w_lr):
    t = x2.shape[0]
    return pl.pallas_call(
        _in_proj_kernel,
        grid=(t // IN_PROJ_ROWS,),
        in_specs=[
            pl.BlockSpec((IN_PROJ_ROWS, D_MODEL), lambda i: (i, 0)),
            pl.BlockSpec((1, D_MODEL), lambda i: (0, 0)),
            pl.BlockSpec((D_MODEL, OFF_GATE_A), lambda i: (0, 0), pipeline_mode=pl.Buffered(1)),
            pl.BlockSpec((D_MODEL, Z_COLS - OFF_GATE_A), lambda i: (0, 0), pipeline_mode=pl.Buffered(1)),
            pl.BlockSpec((D_MODEL, LANE), lambda i: (0, 0)),
        ],
        out_specs=[
            pl.BlockSpec((IN_PROJ_ROWS, Z_COLS), lambda i: (i, 0)),
            pl.BlockSpec((IN_PROJ_ROWS, LANE), lambda i: (i, 0)),
        ],
        out_shape=[
            jax.ShapeDtypeStruct((t, Z_COLS), BF16),
            jax.ShapeDtypeStruct((t, LANE), F32),
        ],
        compiler_params=pltpu.CompilerParams(
            dimension_semantics=("arbitrary",), vmem_limit_bytes=VMEM_LIMIT_BYTES),
        name="in_proj",
    )(x2, g_mix, w_head, w_tail, w_lr)


def _gla_kernel(qk_ref, v_ref, gate_ref, lr_ref, walpha_ref, balpha_ref, gout_ref, o_ref, state_ref):
    nb = qk_ref.shape[0]
    rows, arows = GLA_ROWS, GLA_ATTN_ROWS
    all_rows = nb * rows
    chunk_shift = CHUNK.bit_length() - 1

    @pl.when(pl.program_id(0) == 0)
    def _():
        state_ref[...] = jnp.zeros_like(state_ref)

    def stacked(ref, cols):
        return jnp.concatenate([ref[i, :, cols] for i in range(nb)], axis=0)

    r_id = lax.broadcasted_iota(jnp.int32, (rows, rows), 0)
    c_id = lax.broadcasted_iota(jnp.int32, (rows, rows), 1)
    same_chunk = jnp.right_shift(r_id, chunk_shift) == jnp.right_shift(c_id, chunk_shift)
    cum_and_rest = jnp.concatenate([jnp.where(same_chunk & (r_id >= c_id), 1.0, 0.0),
                                    jnp.where(same_chunk & (r_id < c_id), 1.0, 0.0)], axis=0).astype(BF16)
    ar_id = lax.broadcasted_iota(jnp.int32, (arows, arows), 0)
    ac_id = lax.broadcasted_iota(jnp.int32, (arows, arows), 1)
    attn_same_chunk = jnp.right_shift(ar_id, chunk_shift) == jnp.right_shift(ac_id, chunk_shift)
    attn_causal = ar_id >= ac_id

    logits = _dot(stacked(lr_ref, slice(None)).astype(BF16), walpha_ref[...]) + balpha_ref[...]
    log_a = ((jnp.minimum(logits, 0.0) - jnp.log(1.0 + jnp.exp(-jnp.abs(logits)))) * (1.0 / GLA_TAU)).astype(BF16)
    b_and_rest = [_dot(cum_and_rest, log_a[i * rows:(i + 1) * rows]) for i in range(nb)]
    b_all = jnp.concatenate([br[:rows] for br in b_and_rest], axis=0)
    rest_all = jnp.concatenate([br[rows:] for br in b_and_rest], axis=0)

    q_scale = GLA_DK ** -0.5
    for h in range(GLA_HEADS):
        ks = slice(h * GLA_DK, (h + 1) * GLA_DK)
        vs = slice(h * GLA_DV, (h + 1) * GLA_DV)
        q = stacked(qk_ref, ks)
        k = stacked(qk_ref, slice(GLA_QK + h * GLA_DK, GLA_QK + (h + 1) * GLA_DK))
        v = stacked(v_ref, vs)
        b = b_all[:, ks]
        rest = rest_all[:, ks]
        eb = jnp.exp(b)
        enb = jnp.exp(-b)
        qd = q * (eb * q_scale).astype(BF16)
        kd = k * enb.astype(BF16)
        qi = q * (enb * q_scale).astype(BF16)
        ki = k * eb.astype(BF16)
        k_end = k * jnp.exp(rest).astype(BF16)

        o_intra = []
        for a0 in range(0, all_rows, arows):
            sl = slice(a0, a0 + arows)
            a_past = _dot_nt(qd[sl], kd[sl])
            a_fut = _dot_nt(qi[sl], ki[sl])
            attn = jnp.where(attn_same_chunk, jnp.where(attn_causal, a_past, a_fut), 0.0)
            o_intra.append(_dot(attn.astype(BF16), v[sl]))

        o_inter = []
        for i in range(nb):
            state = state_ref[i, h]
            for c0 in range(i * rows, (i + 1) * rows, CHUNK):
                sl = slice(c0, c0 + CHUNK)
                o_inter.append(_dot_nt(qd[sl], state.astype(BF16)))
                decay = jnp.exp(b[c0:c0 + 1] + rest[c0:c0 + 1])
                state = state * decay + _dot_tn(v[sl], k_end[sl])
            state_ref[i, h] = state

        o = jnp.concatenate(o_intra, axis=0) + jnp.concatenate(o_inter, axis=0)
        o = o * _rms_scale(o) * gout_ref[:, vs]
        gate = stacked(gate_ref, vs).astype(F32)
        out = (o * _silu(gate)).astype(BF16)
        for i in range(nb):
            o_ref[i, :, vs] = out[i * rows:(i + 1) * rows]


def _gla(z3, lr3, w_alpha, b_alpha, g_out):
    bsz, seq, _ = z3.shape
    nblk = D_MODEL
    return pl.pallas_call(
        _gla_kernel,
        grid=(seq // GLA_ROWS,),
        in_specs=[
            pl.BlockSpec((bsz, GLA_ROWS, 2 * GLA_QK), lambda s: (0, s, OFF_QK_A // nblk)),
            pl.BlockSpec((bsz, GLA_ROWS, GLA_V), lambda s: (0, s, OFF_V_A // nblk)),
            pl.BlockSpec((bsz, GLA_ROWS, GLA_V), lambda s: (0, s, OFF_GATE_A // nblk)),
            pl.BlockSpec((bsz, GLA_ROWS, LANE), lambda s: (0, s, 0)),
            pl.BlockSpec((LANE, GLA_QK), lambda s: (0, 0)),
            pl.BlockSpec((1, GLA_QK), lambda s: (0, 0)),
            pl.BlockSpec((1, GLA_V), lambda s: (0, 0)),
        ],
        out_specs=pl.BlockSpec((bsz, GLA_ROWS, GLA_V), lambda s: (0, s, 0)),
        out_shape=jax.ShapeDtypeStruct((bsz, seq, GLA_V), BF16),
        scratch_shapes=[pltpu.VMEM((bsz, GLA_HEADS, GLA_DV, GLA_DK), F32)],
        compiler_params=pltpu.CompilerParams(
            dimension_semantics=("arbitrary",), vmem_limit_bytes=VMEM_LIMIT_BYTES),
        name="gla",
    )(z3, z3, z3, lr3, w_alpha, b_alpha, g_out)


def _sb_kernel(q_ref, k_ref, v_ref, gate_ref, o_ref, acc_ref, spent_ref):
    seq = q_ref.shape[0]
    tq, tk, n_streams = SB_Q_BLOCK, SB_K_BLOCK, SB_STREAMS
    to_log2_logit = (SB_DH ** -0.5) * LOG2_E
    r_id = lax.broadcasted_iota(jnp.int32, (tq, tk), 0)
    c_id = lax.broadcasted_iota(jnp.int32, (tq, tk), 1)
    strictly_before = c_id < r_id
    later_keys = jnp.where(r_id > c_id, 1.0, 0.0).astype(BF16)

    def step(i0, t, on_diagonal, every_stream_has_keys=False):
        log2_betas, deficits, spents, vs = [], [], [], []
        for s in range(n_streams):
            j = i0 + s - t
            q0 = pl.multiple_of((i0 + s) * tq, tq)
            k0 = pl.multiple_of((j if every_stream_has_keys else jnp.maximum(j, 0)) * tk, tk)
            u = _dot_nt(q_ref[pl.ds(q0, tq), :], k_ref[pl.ds(k0, tk), :]) * to_log2_logit
            deficit = jnp.maximum(u, 0.0) + jnp.log2(1.0 + 1.0 / jnp.exp2(jnp.abs(u)))
            log2_betas.append(u - deficit)
            if on_diagonal:
                deficit = jnp.where(strictly_before, deficit, 0.0)
                spents.append(None)
            elif every_stream_has_keys:
                spents.append(spent_ref[s])
            else:
                spents.append(jnp.where(j < 0, SB_NO_KEYS_DEFICIT, spent_ref[s]))
            deficits.append(deficit)
            vs.append(v_ref[pl.ds(k0, tk), :])
        least_spent = None
        for s in range(n_streams):
            later = _dot(deficits[s].astype(BF16), later_keys)
            total = jnp.sum(deficits[s], axis=1, keepdims=True)
            if on_diagonal:
                w = jnp.where(strictly_before, jnp.exp2(log2_betas[s] - later), 0.0)
                acc_ref[s] = _dot(w.astype(BF16), vs[s])
                spent = jnp.broadcast_to(total, (tq, tk))
            else:
                w = jnp.exp2((log2_betas[s] - spents[s]) - later)
                acc_ref[s] += _dot(w.astype(BF16), vs[s])
                spent = spents[s] + total
            spent_ref[s] = spent
            least_spent = spent if least_spent is None else jnp.minimum(least_spent, spent)
        return jnp.min(least_spent)

    def q_group(g, _):
        i0 = g * n_streams
        least_spent = step(i0, 0, True)

        def alive(st):
            return st[1] < SB_DEAD_DEFICIT

        def alive_and_stream0_has_keys(st):
            return jnp.logical_and(alive(st), st[0] <= i0)

        def body_all_keys(st):
            return st[0] + 1, step(i0, st[0], False, every_stream_has_keys=True)

        def body(st):
            return st[0] + 1, step(i0, st[0], False)

        st = lax.while_loop(alive_and_stream0_has_keys, body_all_keys, (jnp.int32(1), least_spent))
        lax.while_loop(alive, body, st)
        for s in range(n_streams):
            q0 = pl.multiple_of((i0 + s) * tq, tq)
            gate = gate_ref[pl.ds(q0, tq), :].astype(F32)
            o_ref[pl.ds(q0, tq), :] = (acc_ref[s] * _silu(gate)).astype(BF16)
        return 0

    lax.fori_loop(0, seq // (tq * n_streams), q_group, 0)


def _stick_breaking(z3):
    bsz, seq, _ = z3.shape

    def spec(off):
        return pl.BlockSpec((None, seq, SB_DH), lambda b, h: (b, 0, off // SB_DH + h))

    return pl.pallas_call(
        _sb_kernel,
        grid=(bsz, SB_HEADS),
        in_specs=[spec(OFF_Q_B), spec(OFF_K_B), spec(OFF_V_B), spec(OFF_GATE_B)],
        out_specs=pl.BlockSpec((None, seq, SB_DH), lambda b, h: (b, 0, h)),
        out_shape=jax.ShapeDtypeStruct((bsz, seq, SB_W), BF16),
        scratch_shapes=[pltpu.VMEM((SB_STREAMS, SB_Q_BLOCK, SB_DH), F32),
                        pltpu.VMEM((SB_STREAMS, SB_Q_BLOCK, SB_K_BLOCK), F32)],
        compiler_params=pltpu.CompilerParams(
            dimension_semantics=("arbitrary", "arbitrary"), vmem_limit_bytes=VMEM_LIMIT_BYTES),
        name="stickbrk",
    )(z3, z3, z3, z3)


def _out_proj_kernel(ga_ref, gb_ref, ma_ref, mb_ref, x_ref, p_ref, wa_ref, wb_ref, wo_ref, wg_ref, wp_ref,
                     gple_ref, gfin_ref, o_ref):
    y_a = _dot(ga_ref[...], wa_ref[...])
    y_b = _dot(gb_ref[...], wb_ref[...])
    merged = _sigmoid(ma_ref[...].astype(F32)) * y_a + _sigmoid(mb_ref[...].astype(F32)) * y_b
    x1 = x_ref[...] + _dot(merged.astype(BF16), wo_ref[...])
    u = (x1 * _rms_scale(x1) * gple_ref[...]).astype(BF16)
    gate_p = _sigmoid(_dot(u, wg_ref[...]))
    x2 = x1 + gate_p * _dot(p_ref[...].astype(BF16), wp_ref[...])
    o_ref[...] = x2 * _rms_scale(x2) * gfin_ref[...]


def _out_proj(ga, gb, z2, x2, p2, wa, wb, wo, wg, wp, g_ple, g_final):
    t = x2.shape[0]
    rows = OUT_PROJ_ROWS

    def row_spec(width, col_block=0):
        return pl.BlockSpec((rows, width), lambda i: (i, col_block))

    def full_spec(shape):
        return pl.BlockSpec(shape, lambda i: (0, 0), pipeline_mode=pl.Buffered(1))

    return pl.pallas_call(
        _out_proj_kernel,
        grid=(t // rows,),
        in_specs=[
            row_spec(GLA_V), row_spec(SB_W),
            row_spec(D_MODEL, OFF_M_A // D_MODEL), row_spec(D_MODEL, OFF_M_B // D_MODEL),
            row_spec(D_MODEL), row_spec(PLE_DIM),
            full_spec((GLA_V, D_MODEL)), full_spec((SB_W, D_MODEL)), full_spec((D_MODEL, D_MODEL)),
            full_spec((D_MODEL, D_MODEL)), full_spec((PLE_DIM, D_MODEL)),
            full_spec((1, D_MODEL)), full_spec((1, D_MODEL)),
        ],
        out_specs=row_spec(D_MODEL),
        out_shape=jax.ShapeDtypeStruct((t, D_MODEL), F32),
        compiler_params=pltpu.CompilerParams(
            dimension_semantics=("arbitrary",), vmem_limit_bytes=VMEM_LIMIT_BYTES),
        name="out_proj",
    )(ga, gb, z2, z2, x2, p2, wa, wb, wo, wg, wp, g_ple, g_final)


def _split_w_in(w):
    lr0 = OFF_GATE_A
    lr1 = lr0 + GLA_LOWRANK
    lr = jnp.pad(w[:, lr0:lr1], ((0, 0), (0, LANE - GLA_LOWRANK)))
    return w[:, :lr0].astype(BF16), w[:, lr1:].astype(BF16), lr.astype(BF16)


def _layer(x, p, g_mix, w_in, w_alpha, b_alpha, g_gla_out, w_out_gla, w_out_sb, w_out, g_ple, w_ple_gate,
           w_ple, g_last):
    bsz, seq, _ = x.shape
    t = bsz * seq
    x2 = x.reshape(t, D_MODEL)
    z, lr = _in_proj(x2, g_mix.reshape(1, D_MODEL), *_split_w_in(w_in))
    z3 = z.reshape(bsz, seq, Z_COLS)
    w_alpha_p = jnp.pad(w_alpha, ((0, LANE - GLA_LOWRANK), (0, 0))).astype(BF16)
    ga = _gla(z3, lr.reshape(bsz, seq, LANE), w_alpha_p, b_alpha.reshape(1, GLA_QK),
              g_gla_out.reshape(1, GLA_V))
    gb = _stick_breaking(z3)
    out = _out_proj(ga.reshape(t, GLA_V), gb.reshape(t, SB_W), z, x2, p.reshape(t, PLE_DIM),
                    w_out_gla.astype(BF16), w_out_sb.astype(BF16), w_out.astype(BF16),
                    w_ple_gate.astype(BF16), w_ple.astype(BF16),
                    g_ple.reshape(1, D_MODEL), g_last.reshape(1, D_MODEL))
    return out.reshape(bsz, seq, D_MODEL)


def kernel(x, p, g_mix, w_in, w_alpha, b_alpha, g_gla_out, w_out_gla, w_out_sb, w_out, g_ple, w_ple_gate, w_ple,
           g_final):
    depth = p.shape[0]
    assert depth == 1, "the fused final RMSNorm assumes a single layer"
    return _layer(x, p[0], g_mix[0], w_in[0], w_alpha[0], b_alpha[0], g_gla_out[0], w_out_gla[0], w_out_sb[0],
                  w_out[0], g_ple[0], w_ple_gate[0], w_ple[0], g_final)
```

```python
import jax
import jax.numpy as jnp
from jax import lax
from jax.experimental import pallas as pl
from jax.experimental.pallas import tpu as pltpu

D_MODEL = 1024
CHUNK = 64
PLE_DIM = 256
GLA_HEADS = 4
GLA_DK = 128
GLA_DV = 256
GLA_LOWRANK = 16
GLA_TAU = 16.0
SB_HEADS = 8
SB_DH = 128
GLA_QK = GLA_HEADS * GLA_DK
GLA_V = GLA_HEADS * GLA_DV
SB_W = SB_HEADS * SB_DH
EPS = 1e-6

LANE = 128

OFF_QK_A = 0
OFF_V_A = OFF_QK_A + 2 * GLA_QK
OFF_GATE_A = OFF_V_A + GLA_V
OFF_Q_B = OFF_GATE_A + GLA_V
OFF_K_B = OFF_Q_B + SB_W
OFF_V_B = OFF_K_B + SB_W
OFF_GATE_B = OFF_V_B + SB_W
OFF_M_A = OFF_GATE_B + SB_W
OFF_M_B = OFF_M_A + D_MODEL
Z_COLS = OFF_M_B + D_MODEL

LOG2_E = 1.4426950408889634
SB_DEAD_DEFICIT = 104.0 * LOG2_E
SB_NO_KEYS_DEFICIT = 1e30
SB_STREAMS = 32

IN_PROJ_ROWS = 512
IN_PROJ_COLS = 1024
SB_Q_BLOCK = 128
SB_K_STEP = 256
OUT_PROJ_ROWS = 1024
GLA_ROWS = 256
GLA_ATTN_ROWS = 2 * CHUNK

VMEM_LIMIT_BYTES = 56 * 1024 * 1024

BF16 = jnp.bfloat16
F32 = jnp.float32


def _dot(a, b):
    return jnp.dot(a, b, preferred_element_type=F32)


def _dot_nt(a, b):
    return lax.dot_general(a, b, (((1,), (1,)), ((), ())), preferred_element_type=F32)


def _dot_tn(a, b):
    return lax.dot_general(a, b, (((0,), (0,)), ((), ())), preferred_element_type=F32)


def _rms_scale(x):
    return lax.rsqrt(jnp.mean(x * x, axis=-1, keepdims=True) + EPS)


def _sigmoid(g):
    return 1.0 / (1.0 + jnp.exp(-g))


def _silu(g):
    return g * _sigmoid(g)


def _in_proj_kernel(x_ref, g_ref, w_head_ref, w_tail_ref, wlr_ref, z_ref, lr_ref):
    x = x_ref[...]
    h = (x * _rms_scale(x) * g_ref[...]).astype(BF16)
    for c in range(0, OFF_GATE_A, IN_PROJ_COLS):
        z_ref[:, c:c + IN_PROJ_COLS] = _dot(h, w_head_ref[:, c:c + IN_PROJ_COLS]).astype(BF16)
    for c in range(0, Z_COLS - OFF_GATE_A, IN_PROJ_COLS):
        z_ref[:, OFF_GATE_A + c:OFF_GATE_A + c + IN_PROJ_COLS] = _dot(
            h, w_tail_ref[:, c:c + IN_PROJ_COLS]).astype(BF16)
    lr_ref[...] = _dot(h, wlr_ref[...])


def _in_proj(x2, g_mix, w_head, w_tail, w_lr):
    t = x2.shape[0]
    return pl.pallas_call(
        _in_proj_kernel,
        grid=(t // IN_PROJ_ROWS,),
        in_specs=[
            pl.BlockSpec((IN_PROJ_ROWS, D_MODEL), lambda i: (i, 0)),
            pl.BlockSpec((1, D_MODEL), lambda i: (0, 0)),
            pl.BlockSpec((D_MODEL, OFF_GATE_A), lambda i: (0, 0), pipeline_mode=pl.Buffered(1)),
            pl.BlockSpec((D_MODEL, Z_COLS - OFF_GATE_A), lambda i: (0, 0), pipeline_mode=pl.Buffered(1)),
            pl.BlockSpec((D_MODEL, LANE), lambda i: (0, 0)),
        ],
        out_specs=[
            pl.BlockSpec((IN_PROJ_ROWS, Z_COLS), lambda i: (i, 0)),
            pl.BlockSpec((IN_PROJ_ROWS, LANE), lambda i: (i, 0)),
        ],
        out_shape=[
            jax.ShapeDtypeStruct((t, Z_COLS), BF16),
            jax.ShapeDtypeStruct((t, LANE), F32),
        ],
        compiler_params=pltpu.CompilerParams(
            dimension_semantics=("arbitrary",), vmem_limit_bytes=VMEM_LIMIT_BYTES),
        name="in_proj",
    )(x2, g_mix, w_head, w_tail, w_lr)


def _gla_kernel(qk_ref, v_ref, gate_ref, lr_ref, walpha_ref, balpha_ref, gout_ref, o_ref, state_ref):
    rows, arows = GLA_ROWS, GLA_ATTN_ROWS
    chunk_shift = CHUNK.bit_length() - 1

    @pl.when(pl.program_id(1) == 0)
    def _():
        state_ref[...] = jnp.zeros_like(state_ref)

    r_id = lax.broadcasted_iota(jnp.int32, (rows, rows), 0)
    c_id = lax.broadcasted_iota(jnp.int32, (rows, rows), 1)
    same_chunk = jnp.right_shift(r_id, chunk_shift) == jnp.right_shift(c_id, chunk_shift)
    cum_and_rest = jnp.concatenate([jnp.where(same_chunk & (r_id >= c_id), 1.0, 0.0),
                                    jnp.where(same_chunk & (r_id < c_id), 1.0, 0.0)], axis=0).astype(BF16)
    ar_id = lax.broadcasted_iota(jnp.int32, (arows, arows), 0)
    ac_id = lax.broadcasted_iota(jnp.int32, (arows, arows), 1)
    attn_same_chunk = jnp.right_shift(ar_id, chunk_shift) == jnp.right_shift(ac_id, chunk_shift)
    attn_causal = ar_id >= ac_id

    logits = _dot(lr_ref[...].astype(BF16), walpha_ref[...]) + balpha_ref[...]
    log_a = (jnp.minimum(logits, 0.0) - jnp.log(1.0 + jnp.exp(-jnp.abs(logits)))) * (1.0 / GLA_TAU)
    b_and_rest = _dot(cum_and_rest, log_a.astype(BF16))

    q_scale = GLA_DK ** -0.5
    for h in range(GLA_HEADS):
        ks = slice(h * GLA_DK, (h + 1) * GLA_DK)
        vs = slice(h * GLA_DV, (h + 1) * GLA_DV)
        q = qk_ref[:, h * GLA_DK:(h + 1) * GLA_DK]
        k = qk_ref[:, GLA_QK + h * GLA_DK:GLA_QK + (h + 1) * GLA_DK]
        v = v_ref[:, vs]
        b = b_and_rest[:rows, ks]
        rest = b_and_rest[rows:, ks]
        eb = jnp.exp(b)
        enb = jnp.exp(-b)
        qd = q * (eb * q_scale).astype(BF16)
        kd = k * enb.astype(BF16)
        qi = q * (enb * q_scale).astype(BF16)
        ki = k * eb.astype(BF16)
        k_end = k * jnp.exp(rest).astype(BF16)

        o_intra = []
        for a0 in range(0, rows, arows):
            sl = slice(a0, a0 + arows)
            a_past = _dot_nt(qd[sl], kd[sl])
            a_fut = _dot_nt(qi[sl], ki[sl])
            attn = jnp.where(attn_same_chunk, jnp.where(attn_causal, a_past, a_fut), 0.0)
            o_intra.append(_dot(attn.astype(BF16), v[sl]))

        state = state_ref[h]
        o_inter = []
        for c0 in range(0, rows, CHUNK):
            sl = slice(c0, c0 + CHUNK)
            o_inter.append(_dot_nt(qd[sl], state.astype(BF16)))
            decay = jnp.exp(b[c0:c0 + 1] + rest[c0:c0 + 1])
            state = state * decay + _dot_tn(v[sl], k_end[sl])
        state_ref[h] = state

        o = jnp.concatenate(o_intra, axis=0) + jnp.concatenate(o_inter, axis=0)
        o = o * _rms_scale(o) * gout_ref[:, vs]
        o_ref[:, vs] = (o * _silu(gate_ref[:, vs].astype(F32))).astype(BF16)


def _gla(z3, lr3, w_alpha, b_alpha, g_out):
    bsz, seq, _ = z3.shape
    nblk = D_MODEL
    return pl.pallas_call(
        _gla_kernel,
        grid=(bsz, seq // GLA_ROWS),
        in_specs=[
            pl.BlockSpec((None, GLA_ROWS, 2 * GLA_QK), lambda b, s: (b, s, OFF_QK_A // nblk)),
            pl.BlockSpec((None, GLA_ROWS, GLA_V), lambda b, s: (b, s, OFF_V_A // nblk)),
            pl.BlockSpec((None, GLA_ROWS, GLA_V), lambda b, s: (b, s, OFF_GATE_A // nblk)),
            pl.BlockSpec((None, GLA_ROWS, LANE), lambda b, s: (b, s, 0)),
            pl.BlockSpec((LANE, GLA_QK), lambda b, s: (0, 0)),
            pl.BlockSpec((1, GLA_QK), lambda b, s: (0, 0)),
            pl.BlockSpec((1, GLA_V), lambda b, s: (0, 0)),
        ],
        out_specs=pl.BlockSpec((None, GLA_ROWS, GLA_V), lambda b, s: (b, s, 0)),
        out_shape=jax.ShapeDtypeStruct((bsz, seq, GLA_V), BF16),
        scratch_shapes=[pltpu.VMEM((GLA_HEADS, GLA_DV, GLA_DK), F32)],
        compiler_params=pltpu.CompilerParams(
            dimension_semantics=("arbitrary", "arbitrary"), vmem_limit_bytes=VMEM_LIMIT_BYTES),
        name="gla",
    )(z3, z3, z3, lr3, w_alpha, b_alpha, g_out)


def _sb_kernel(q_ref, k_ref, v_ref, gate_ref, o_ref, acc_ref, spent_ref):
    seq = q_ref.shape[0]
    tq, kstep, n_streams = SB_Q_BLOCK, SB_K_STEP, SB_STREAMS
    to_log2_logit = (SB_DH ** -0.5) * LOG2_E

    def later_keys_matrix(n):
        r_id = lax.broadcasted_iota(jnp.int32, (n, n), 0)
        c_id = lax.broadcasted_iota(jnp.int32, (n, n), 1)
        return jnp.where(r_id > c_id, 1.0, 0.0).astype(BF16)

    later_keys, later_keys_diagonal = later_keys_matrix(kstep), later_keys_matrix(tq)
    strictly_before = lax.broadcasted_iota(jnp.int32, (tq, tq), 1) < lax.broadcasted_iota(jnp.int32, (tq, tq), 0)
    lane_id = lax.broadcasted_iota(jnp.int32, (tq, kstep), 1)

    def log2_terms(q0, k):
        u = _dot_nt(q_ref[pl.ds(q0, tq), :], k) * to_log2_logit
        deficit = jnp.maximum(u, 0.0) + jnp.log2(1.0 + 1.0 / jnp.exp2(jnp.abs(u)))
        return u - deficit, deficit

    def diagonal_step(i0):
        log2_betas, deficits = [], []
        for s in range(n_streams):
            q0 = pl.multiple_of((i0 + s) * tq, tq)
            log2_beta, deficit = log2_terms(q0, k_ref[pl.ds(q0, tq), :])
            log2_betas.append(log2_beta)
            deficits.append(jnp.where(strictly_before, deficit, 0.0))
        least_spent = None
        for s in range(n_streams):
            q0 = pl.multiple_of((i0 + s) * tq, tq)
            later = _dot(deficits[s].astype(BF16), later_keys_diagonal)
            w = jnp.where(strictly_before, jnp.exp2(log2_betas[s] - later), 0.0)
            acc_ref[s] = _dot(w.astype(BF16), v_ref[pl.ds(q0, tq), :])
            spent = jnp.broadcast_to(jnp.sum(deficits[s], axis=1, keepdims=True), (tq, tq))
            spent_ref[s] = spent
            least_spent = spent if least_spent is None else jnp.minimum(least_spent, spent)
        return jnp.min(least_spent)

    def key_step(i0, t, every_stream_has_keys):
        log2_betas, deficits, spents, vs, valids = [], [], [], [], []
        for s in range(n_streams):
            q0 = pl.multiple_of((i0 + s) * tq, tq)
            k0 = (i0 + s) * tq - t * kstep
            if every_stream_has_keys:
                k0 = pl.multiple_of(k0, tq)
                k, v = k_ref[pl.ds(k0, kstep), :], v_ref[pl.ds(k0, kstep), :]
                spent, valid = spent_ref[s], None
            else:
                starts = [pl.multiple_of(jnp.maximum(k0 + off, 0), tq) for off in range(0, kstep, tq)]
                k = jnp.concatenate([k_ref[pl.ds(st, tq), :] for st in starts], axis=0)
                v = jnp.concatenate([v_ref[pl.ds(st, tq), :] for st in starts], axis=0)
                spent = jnp.where(k0 + kstep <= 0, SB_NO_KEYS_DEFICIT, spent_ref[s])
                valid = lane_id >= -k0
            log2_beta, deficit = log2_terms(q0, k)
            if valid is not None:
                deficit = jnp.where(valid, deficit, 0.0)
            log2_betas.append(log2_beta)
            deficits.append(deficit)
            spents.append(spent)
            vs.append(v)
            valids.append(valid)
        least_spent = None
        for s in range(n_streams):
            later = _dot(deficits[s].astype(BF16), later_keys)
            spent_wide = jnp.concatenate([spents[s]] * (kstep // tq), axis=1)
            w = jnp.exp2((log2_betas[s] - spent_wide) - later)
            if valids[s] is not None:
                w = jnp.where(valids[s], w, 0.0)
            acc_ref[s] += _dot(w.astype(BF16), vs[s])
            spent = spents[s] + jnp.sum(deficits[s], axis=1, keepdims=True)
            spent_ref[s] = spent
            least_spent = spent if least_spent is None else jnp.minimum(least_spent, spent)
        return jnp.min(least_spent)

    def q_group(g, _):
        i0 = g * n_streams
        least_spent = diagonal_step(i0)

        def alive(st):
            return st[1] < SB_DEAD_DEFICIT

        def alive_and_every_stream_has_keys(st):
            return jnp.logical_and(alive(st), st[0] * kstep <= i0 * tq)

        def body_all_keys(st):
            return st[0] + 1, key_step(i0, st[0], True)

        def body(st):
            return st[0] + 1, key_step(i0, st[0], False)

        st = lax.while_loop(alive_and_every_stream_has_keys, body_all_keys, (jnp.int32(1), least_spent))
        lax.while_loop(alive, body, st)
        for s in range(n_streams):
            q0 = pl.multiple_of((i0 + s) * tq, tq)
            gate = gate_ref[pl.ds(q0, tq), :].astype(F32)
            o_ref[pl.ds(q0, tq), :] = (acc_ref[s] * _silu(gate)).astype(BF16)
        return 0

    lax.fori_loop(0, seq // (tq * n_streams), q_group, 0)


def _stick_breaking(z3):
    bsz, seq, _ = z3.shape

    def spec(off):
        return pl.BlockSpec((None, seq, SB_DH), lambda b, h: (b, 0, off // SB_DH + h))

    return pl.pallas_call(
        _sb_kernel,
        grid=(bsz, SB_HEADS),
        in_specs=[spec(OFF_Q_B), spec(OFF_K_B), spec(OFF_V_B), spec(OFF_GATE_B)],
        out_specs=pl.BlockSpec((None, seq, SB_DH), lambda b, h: (b, 0, h)),
        out_shape=jax.ShapeDtypeStruct((bsz, seq, SB_W), BF16),
        scratch_shapes=[pltpu.VMEM((SB_STREAMS, SB_Q_BLOCK, SB_DH), F32),
                        pltpu.VMEM((SB_STREAMS, SB_Q_BLOCK, SB_Q_BLOCK), F32)],
        compiler_params=pltpu.CompilerParams(
            dimension_semantics=("arbitrary", "arbitrary"), vmem_limit_bytes=VMEM_LIMIT_BYTES),
        name="stickbrk",
    )(z3, z3, z3, z3)


def _out_proj_kernel(ga_ref, gb_ref, ma_ref, mb_ref, x_ref, p_ref, wa_ref, wb_ref, wo_ref, wg_ref, wp_ref,
                     gple_ref, gfin_ref, o_ref):
    y_a = _dot(ga_ref[...], wa_ref[...])
    y_b = _dot(gb_ref[...], wb_ref[...])
    merged = _sigmoid(ma_ref[...].astype(F32)) * y_a + _sigmoid(mb_ref[...].astype(F32)) * y_b
    x1 = x_ref[...] + _dot(merged.astype(BF16), wo_ref[...])
    u = (x1 * _rms_scale(x1) * gple_ref[...]).astype(BF16)
    gate_p = _sigmoid(_dot(u, wg_ref[...]))
    x2 = x1 + gate_p * _dot(p_ref[...].astype(BF16), wp_ref[...])
    o_ref[...] = x2 * _rms_scale(x2) * gfin_ref[...]


def _out_proj(ga, gb, z2, x2, p2, wa, wb, wo, wg, wp, g_ple, g_final):
    t = x2.shape[0]
    rows = OUT_PROJ_ROWS

    def row_spec(width, col_block=0):
        return pl.BlockSpec((rows, width), lambda i: (i, col_block))

    def full_spec(shape):
        return pl.BlockSpec(shape, lambda i: (0, 0), pipeline_mode=pl.Buffered(1))

    return pl.pallas_call(
        _out_proj_kernel,
        grid=(t // rows,),
        in_specs=[
            row_spec(GLA_V), row_spec(SB_W),
            row_spec(D_MODEL, OFF_M_A // D_MODEL), row_spec(D_MODEL, OFF_M_B // D_MODEL),
            row_spec(D_MODEL), row_spec(PLE_DIM),
            full_spec((GLA_V, D_MODEL)), full_spec((SB_W, D_MODEL)), full_spec((D_MODEL, D_MODEL)),
            full_spec((D_MODEL, D_MODEL)), full_spec((PLE_DIM, D_MODEL)),
            full_spec((1, D_MODEL)), full_spec((1, D_MODEL)),
        ],
        out_specs=row_spec(D_MODEL),
        out_shape=jax.ShapeDtypeStruct((t, D_MODEL), F32),
        compiler_params=pltpu.CompilerParams(
            dimension_semantics=("arbitrary",), vmem_limit_bytes=VMEM_LIMIT_BYTES),
        name="out_proj",
    )(ga, gb, z2, z2, x2, p2, wa, wb, wo, wg, wp, g_ple, g_final)


def _split_w_in(w):
    lr0 = OFF_GATE_A
    lr1 = lr0 + GLA_LOWRANK
    lr = jnp.pad(w[:, lr0:lr1], ((0, 0), (0, LANE - GLA_LOWRANK)))
    return w[:, :lr0].astype(BF16), w[:, lr1:].astype(BF16), lr.astype(BF16)


def _layer(x, p, g_mix, w_in, w_alpha, b_alpha, g_gla_out, w_out_gla, w_out_sb, w_out, g_ple, w_ple_gate,
           w_ple, g_last):
    bsz, seq, _ = x.shape
    t = bsz * seq
    x2 = x.reshape(t, D_MODEL)
    z, lr = _in_proj(x2, g_mix.reshape(1, D_MODEL), *_split_w_in(w_in))
    z3 = z.reshape(bsz, seq, Z_COLS)
    w_alpha_p = jnp.pad(w_alpha, ((0, LANE - GLA_LOWRANK), (0, 0))).astype(BF16)
    ga = _gla(z3, lr.reshape(bsz, seq, LANE), w_alpha_p, b_alpha.reshape(1, GLA_QK),
              g_gla_out.reshape(1, GLA_V))
    gb = _stick_breaking(z3)
    out = _out_proj(ga.reshape(t, GLA_V), gb.reshape(t, SB_W), z, x2, p.reshape(t, PLE_DIM),
                    w_out_gla.astype(BF16), w_out_sb.astype(BF16), w_out.astype(BF16),
                    w_ple_gate.astype(BF16), w_ple.astype(BF16),
                    g_ple.reshape(1, D_MODEL), g_last.reshape(1, D_MODEL))
    return out.reshape(bsz, seq, D_MODEL)


def kernel(x, p, g_mix, w_in, w_alpha, b_alpha, g_gla_out, w_out_gla, w_out_sb, w_out, g_ple, w_ple_gate, w_ple,
           g_final):
    depth = p.shape[0]
    assert depth == 1, "the fused final RMSNorm assumes a single layer"
    return _layer(x, p[0], g_mix[0], w_in[0], w_alpha[0], b_alpha[0], g_gla_out[0], w_out_gla[0], w_out_sb[0],
                  w_out[0], g_ple[0], w_ple_gate[0], w_ple[0], g_final)
```

```python
import jax
import jax.numpy as jnp
from jax import lax
from jax.experimental import pallas as pl
from jax.experimental.pallas import tpu as pltpu

D_MODEL = 1024
CHUNK = 64
PLE_DIM = 256
GLA_HEADS = 4
GLA_DK = 128
GLA_DV = 256
GLA_LOWRANK = 16
GLA_TAU = 16.0
SB_HEADS = 8
SB_DH = 128
GLA_QK = GLA_HEADS * GLA_DK
GLA_V = GLA_HEADS * GLA_DV
SB_W = SB_HEADS * SB_DH
EPS = 1e-6

LANE = 128

OFF_QK_A = 0
OFF_V_A = OFF_QK_A + 2 * GLA_QK
OFF_GATE_A = OFF_V_A + GLA_V
OFF_Q_B = OFF_GATE_A + GLA_V
OFF_K_B = OFF_Q_B + SB_W
OFF_V_B = OFF_K_B + SB_W
OFF_GATE_B = OFF_V_B + SB_W
OFF_M_A = OFF_GATE_B + SB_W
OFF_M_B = OFF_M_A + D_MODEL
Z_COLS = OFF_M_B + D_MODEL

LOG2_E = 1.4426950408889634
SB_DEAD_DEFICIT = 104.0 * LOG2_E
SB_NO_KEYS_DEFICIT = 1e30
SB_STREAMS = 32

IN_PROJ_ROWS = 512
IN_PROJ_COLS = 1024
SB_Q_BLOCK = 128
SB_K_STEP = 256
OUT_PROJ_ROWS = 1024
GLA_ROWS = 256
GLA_ATTN_ROWS = 2 * CHUNK

VMEM_LIMIT_BYTES = 56 * 1024 * 1024

BF16 = jnp.bfloat16
F32 = jnp.float32


def _dot(a, b):
    return jnp.dot(a, b, preferred_element_type=F32)


def _dot_nt(a, b):
    return lax.dot_general(a, b, (((1,), (1,)), ((), ())), preferred_element_type=F32)


def _dot_tn(a, b):
    return lax.dot_general(a, b, (((0,), (0,)), ((), ())), preferred_element_type=F32)


def _rms_scale(x):
    return lax.rsqrt(jnp.mean(x * x, axis=-1, keepdims=True) + EPS)


def _sigmoid(g):
    return 1.0 / (1.0 + jnp.exp(-g))


def _silu(g):
    return g * _sigmoid(g)


def _in_proj_kernel(x_ref, g_ref, w_head_ref, w_tail_ref, wlr_ref, z_ref, lr_ref):
    x = x_ref[...]
    h = (x * _rms_scale(x) * g_ref[...]).astype(BF16)
    for c in range(0, OFF_GATE_A, IN_PROJ_COLS):
        z_ref[:, c:c + IN_PROJ_COLS] = _dot(h, w_head_ref[:, c:c + IN_PROJ_COLS]).astype(BF16)
    for c in range(0, Z_COLS - OFF_GATE_A, IN_PROJ_COLS):
        z_ref[:, OFF_GATE_A + c:OFF_GATE_A + c + IN_PROJ_COLS] = _dot(
            h, w_tail_ref[:, c:c + IN_PROJ_COLS]).astype(BF16)
    lr_ref[...] = _dot(h, wlr_ref[...])


def _in_proj(x2, g_mix, w_head, w_tail, w_lr):
    t = x2.shape[0]
    return pl.pallas_call(
        _in_proj_kernel,
        grid=(t // IN_PROJ_ROWS,),
        in_specs=[
            pl.BlockSpec((IN_PROJ_ROWS, D_MODEL), lambda i: (i, 0)),
            pl.BlockSpec((1, D_MODEL), lambda i: (0, 0)),
            pl.BlockSpec((D_MODEL, OFF_GATE_A), lambda i: (0, 0), pipeline_mode=pl.Buffered(1)),
            pl.BlockSpec((D_MODEL, Z_COLS - OFF_GATE_A), lambda i: (0, 0), pipeline_mode=pl.Buffered(1)),
            pl.BlockSpec((D_MODEL, LANE), lambda i: (0, 0)),
        ],
        out_specs=[
            pl.BlockSpec((IN_PROJ_ROWS, Z_COLS), lambda i: (i, 0)),
            pl.BlockSpec((IN_PROJ_ROWS, LANE), lambda i: (i, 0)),
        ],
        out_shape=[
            jax.ShapeDtypeStruct((t, Z_COLS), BF16),
            jax.ShapeDtypeStruct((t, LANE), F32),
        ],
        compiler_params=pltpu.CompilerParams(
            dimension_semantics=("arbitrary",), vmem_limit_bytes=VMEM_LIMIT_BYTES),
        name="in_proj",
    )(x2, g_mix, w_head, w_tail, w_lr)


def _gla_kernel(qk_ref, v_ref, gate_ref, lr_ref, walpha_ref, balpha_ref, gout_ref, o_ref, state_ref):
    rows, arows = GLA_ROWS, GLA_ATTN_ROWS
    chunk_shift = CHUNK.bit_length() - 1

    @pl.when(pl.program_id(1) == 0)
    def _():
        state_ref[...] = jnp.zeros_like(state_ref)

    r_id = lax.broadcasted_iota(jnp.int32, (rows, rows), 0)
    c_id = lax.broadcasted_iota(jnp.int32, (rows, rows), 1)
    same_chunk = jnp.right_shift(r_id, chunk_shift) == jnp.right_shift(c_id, chunk_shift)
    cum_and_rest = jnp.concatenate([jnp.where(same_chunk & (r_id >= c_id), 1.0, 0.0),
                                    jnp.where(same_chunk & (r_id < c_id), 1.0, 0.0)], axis=0).astype(BF16)
    ar_id = lax.broadcasted_iota(jnp.int32, (arows, arows), 0)
    ac_id = lax.broadcasted_iota(jnp.int32, (arows, arows), 1)
    attn_same_chunk = jnp.right_shift(ar_id, chunk_shift) == jnp.right_shift(ac_id, chunk_shift)
    attn_causal = ar_id >= ac_id

    logits = _dot(lr_ref[...].astype(BF16), walpha_ref[...]) + balpha_ref[...]
    log_a = (jnp.minimum(logits, 0.0) - jnp.log(1.0 + jnp.exp(-jnp.abs(logits)))) * (1.0 / GLA_TAU)
    b_and_rest = _dot(cum_and_rest, log_a.astype(BF16))

    q_scale = GLA_DK ** -0.5
    for h in range(GLA_HEADS):
        ks = slice(h * GLA_DK, (h + 1) * GLA_DK)
        vs = slice(h * GLA_DV, (h + 1) * GLA_DV)
        q = qk_ref[:, h * GLA_DK:(h + 1) * GLA_DK]
        k = qk_ref[:, GLA_QK + h * GLA_DK:GLA_QK + (h + 1) * GLA_DK]
        v = v_ref[:, vs]
        b = b_and_rest[:rows, ks]
        rest = b_and_rest[rows:, ks]
        eb = jnp.exp(b)
        enb = jnp.exp(-b)
        qd = q * (eb * q_scale).astype(BF16)
        kd = k * enb.astype(BF16)
        qi = q * (enb * q_scale).astype(BF16)
        ki = k * eb.astype(BF16)
        k_end = k * jnp.exp(rest).astype(BF16)

        o_intra = []
        for a0 in range(0, rows, arows):
            sl = slice(a0, a0 + arows)
            a_past = _dot_nt(qd[sl], kd[sl])
            a_fut = _dot_nt(qi[sl], ki[sl])
            attn = jnp.where(attn_same_chunk, jnp.where(attn_causal, a_past, a_fut), 0.0)
            o_intra.append(_dot(attn.astype(BF16), v[sl]))

        state = state_ref[h]
        o_inter = []
        for c0 in range(0, rows, CHUNK):
            sl = slice(c0, c0 + CHUNK)
            o_inter.append(_dot_nt(qd[sl], state.astype(BF16)))
            decay = jnp.exp(b[c0:c0 + 1] + rest[c0:c0 + 1])
            state = state * decay + _dot_tn(v[sl], k_end[sl])
        state_ref[h] = state

        o = jnp.concatenate(o_intra, axis=0) + jnp.concatenate(o_inter, axis=0)
        o = o * _rms_scale(o) * gout_ref[:, vs]
        o_ref[:, vs] = (o * _silu(gate_ref[:, vs].astype(F32))).astype(BF16)


def _gla(z3, lr3, w_alpha, b_alpha, g_out):
    bsz, seq, _ = z3.shape
    nblk = D_MODEL
    return pl.pallas_call(
        _gla_kernel,
        grid=(bsz, seq // GLA_ROWS),
        in_specs=[
            pl.BlockSpec((None, GLA_ROWS, 2 * GLA_QK), lambda b, s: (b, s, OFF_QK_A // nblk)),
            pl.BlockSpec((None, GLA_ROWS, GLA_V), lambda b, s: (b, s, OFF_V_A // nblk)),
            pl.BlockSpec((None, GLA_ROWS, GLA_V), lambda b, s: (b, s, OFF_GATE_A // nblk)),
            pl.BlockSpec((None, GLA_ROWS, LANE), lambda b, s: (b, s, 0)),
            pl.BlockSpec((LANE, GLA_QK), lambda b, s: (0, 0)),
            pl.BlockSpec((1, GLA_QK), lambda b, s: (0, 0)),
            pl.BlockSpec((1, GLA_V), lambda b, s: (0, 0)),
        ],
        out_specs=pl.BlockSpec((None, GLA_ROWS, GLA_V), lambda b, s: (b, s, 0)),
        out_shape=jax.ShapeDtypeStruct((bsz, seq, GLA_V), BF16),
        scratch_shapes=[pltpu.VMEM((GLA_HEADS, GLA_DV, GLA_DK), F32)],
        compiler_params=pltpu.CompilerParams(
            dimension_semantics=("arbitrary", "arbitrary"), vmem_limit_bytes=VMEM_LIMIT_BYTES),
        name="gla",
    )(z3, z3, z3, lr3, w_alpha, b_alpha, g_out)


def _sb_kernel(q_ref, k_ref, v_ref, gate_ref, o_ref, acc_ref, spent_ref):
    seq = q_ref.shape[0]
    tq, kstep, n_streams = SB_Q_BLOCK, SB_K_STEP, SB_STREAMS
    to_log2_logit = (SB_DH ** -0.5) * LOG2_E

    def later_keys_matrix(n):
        r_id = lax.broadcasted_iota(jnp.int32, (n, n), 0)
        c_id = lax.broadcasted_iota(jnp.int32, (n, n), 1)
        return jnp.where(r_id > c_id, 1.0, 0.0).astype(BF16)

    later_keys, later_keys_diagonal = later_keys_matrix(kstep), later_keys_matrix(tq)
    strictly_before = lax.broadcasted_iota(jnp.int32, (tq, tq), 1) < lax.broadcasted_iota(jnp.int32, (tq, tq), 0)
    lane_id = lax.broadcasted_iota(jnp.int32, (tq, kstep), 1)

    def log2_terms(q0, k):
        u = _dot_nt(q_ref[pl.ds(q0, tq), :], k) * to_log2_logit
        deficit = jnp.maximum(u, 0.0) + jnp.log2(1.0 + 1.0 / jnp.exp2(jnp.abs(u)))
        return u - deficit, deficit

    def diagonal_step(i0):
        log2_betas, deficits = [], []
        for s in range(n_streams):
            q0 = pl.multiple_of((i0 + s) * tq, tq)
            log2_beta, deficit = log2_terms(q0, k_ref[pl.ds(q0, tq), :])
            log2_betas.append(log2_beta)
            deficits.append(jnp.where(strictly_before, deficit, 0.0))
        least_spent = None
        for s in range(n_streams):
            q0 = pl.multiple_of((i0 + s) * tq, tq)
            later = _dot(deficits[s].astype(BF16), later_keys_diagonal)
            w = jnp.where(strictly_before, jnp.exp2(log2_betas[s] - later), 0.0)
            acc_ref[s] = _dot(w.astype(BF16), v_ref[pl.ds(q0, tq), :])
            spent = jnp.broadcast_to(jnp.sum(deficits[s], axis=1, keepdims=True), (tq, tq))
            spent_ref[s] = spent
            least_spent = spent if least_spent is None else jnp.minimum(least_spent, spent)
        return jnp.min(least_spent)

    def key_step(i0, t, every_stream_has_keys):
        log2_betas, deficits, spents, vs, valids, k0s = [], [], [], [], [], []
        for s in range(n_streams):
            q0 = pl.multiple_of((i0 + s) * tq, tq)
            k0 = (i0 + s) * tq - t * kstep
            k0s.append(k0)
            if every_stream_has_keys:
                k0 = pl.multiple_of(k0, tq)
                k, v = k_ref[pl.ds(k0, kstep), :], v_ref[pl.ds(k0, kstep), :]
                spent, valid = spent_ref[s], None
            else:
                starts = [pl.multiple_of(jnp.maximum(k0 + off, 0), tq) for off in range(0, kstep, tq)]
                k = jnp.concatenate([k_ref[pl.ds(st, tq), :] for st in starts], axis=0)
                v = jnp.concatenate([v_ref[pl.ds(st, tq), :] for st in starts], axis=0)
                spent = jnp.where(k0 + kstep <= 0, SB_NO_KEYS_DEFICIT, spent_ref[s])
                valid = lane_id >= -k0
            log2_beta, deficit = log2_terms(q0, k)
            if valid is not None:
                deficit = jnp.where(valid, deficit, 0.0)
            log2_betas.append(log2_beta)
            deficits.append(deficit)
            spents.append(spent)
            vs.append(v)
            valids.append(valid)
        least_spent = None
        for s in range(n_streams):
            later = _dot(deficits[s].astype(BF16), later_keys)
            spent_wide = jnp.concatenate([spents[s]] * (kstep // tq), axis=1)
            w = jnp.exp2((log2_betas[s] - spent_wide) - later)
            if valids[s] is not None:
                w = jnp.where(valids[s], w, 0.0)
            acc_ref[s] += _dot(w.astype(BF16), vs[s])
            spent = spents[s] + jnp.sum(deficits[s], axis=1, keepdims=True)
            if valids[s] is not None:
                spent = jnp.where(k0s[s] <= 0, SB_NO_KEYS_DEFICIT, spent)
            spent_ref[s] = spent
            least_spent = spent if least_spent is None else jnp.minimum(least_spent, spent)
        return jnp.min(least_spent)

    def q_group(g, _):
        i0 = g * n_streams
        least_spent = diagonal_step(i0)

        def alive(st):
            return st[1] < SB_DEAD_DEFICIT

        def alive_and_every_stream_has_keys(st):
            return jnp.logical_and(alive(st), st[0] * kstep <= i0 * tq)

        def body_all_keys(st):
            return st[0] + 1, key_step(i0, st[0], True)

        def body(st):
            return st[0] + 1, key_step(i0, st[0], False)

        st = lax.while_loop(alive_and_every_stream_has_keys, body_all_keys, (jnp.int32(1), least_spent))
        lax.while_loop(alive, body, st)
        for s in range(n_streams):
            q0 = pl.multiple_of((i0 + s) * tq, tq)
            gate = gate_ref[pl.ds(q0, tq), :].astype(F32)
            o_ref[pl.ds(q0, tq), :] = (acc_ref[s] * _silu(gate)).astype(BF16)
        return 0

    lax.fori_loop(0, seq // (tq * n_streams), q_group, 0)


def _stick_breaking(z3):
    bsz, seq, _ = z3.shape

    def spec(off):
        return pl.BlockSpec((None, seq, SB_DH), lambda b, h: (b, 0, off // SB_DH + h))

    return pl.pallas_call(
        _sb_kernel,
        grid=(bsz, SB_HEADS),
        in_specs=[spec(OFF_Q_B), spec(OFF_K_B), spec(OFF_V_B), spec(OFF_GATE_B)],
        out_specs=pl.BlockSpec((None, seq, SB_DH), lambda b, h: (b, 0, h)),
        out_shape=jax.ShapeDtypeStruct((bsz, seq, SB_W), BF16),
        scratch_shapes=[pltpu.VMEM((SB_STREAMS, SB_Q_BLOCK, SB_DH), F32),
                        pltpu.VMEM((SB_STREAMS, SB_Q_BLOCK, SB_Q_BLOCK), F32)],
        compiler_params=pltpu.CompilerParams(
            dimension_semantics=("arbitrary", "arbitrary"), vmem_limit_bytes=VMEM_LIMIT_BYTES),
        name="stickbrk",
    )(z3, z3, z3, z3)


def _out_proj_kernel(ga_ref, gb_ref, ma_ref, mb_ref, x_ref, p_ref, wa_ref, wb_ref, wo_ref, wg_ref, wp_ref,
                     gple_ref, gfin_ref, o_ref):
    y_a = _dot(ga_ref[...], wa_ref[...])
    y_b = _dot(gb_ref[...], wb_ref[...])
    merged = _sigmoid(ma_ref[...].astype(F32)) * y_a + _sigmoid(mb_ref[...].astype(F32)) * y_b
    x1 = x_ref[...] + _dot(merged.astype(BF16), wo_ref[...])
    u = (x1 * _rms_scale(x1) * gple_ref[...]).astype(BF16)
    gate_p = _sigmoid(_dot(u, wg_ref[...]))
    x2 = x1 + gate_p * _dot(p_ref[...].astype(BF16), wp_ref[...])
    o_ref[...] = x2 * _rms_scale(x2) * gfin_ref[...]


def _out_proj(ga, gb, z2, x2, p2, wa, wb, wo, wg, wp, g_ple, g_final):
    t = x2.shape[0]
    rows = OUT_PROJ_ROWS

    def row_spec(width, col_block=0):
        return pl.BlockSpec((rows, width), lambda i: (i, col_block))

    def full_spec(shape):
        return pl.BlockSpec(shape, lambda i: (0, 0), pipeline_mode=pl.Buffered(1))

    return pl.pallas_call(
        _out_proj_kernel,
        grid=(t // rows,),
        in_specs=[
            row_spec(GLA_V), row_spec(SB_W),
            row_spec(D_MODEL, OFF_M_A // D_MODEL), row_spec(D_MODEL, OFF_M_B // D_MODEL),
            row_spec(D_MODEL), row_spec(PLE_DIM),
            full_spec((GLA_V, D_MODEL)), full_spec((SB_W, D_MODEL)), full_spec((D_MODEL, D_MODEL)),
            full_spec((D_MODEL, D_MODEL)), full_spec((PLE_DIM, D_MODEL)),
            full_spec((1, D_MODEL)), full_spec((1, D_MODEL)),
        ],
        out_specs=row_spec(D_MODEL),
        out_shape=jax.ShapeDtypeStruct((t, D_MODEL), F32),
        compiler_params=pltpu.CompilerParams(
            dimension_semantics=("arbitrary",), vmem_limit_bytes=VMEM_LIMIT_BYTES),
        name="out_proj",
    )(ga, gb, z2, z2, x2, p2, wa, wb, wo, wg, wp, g_ple, g_final)


def _split_w_in(w):
    lr0 = OFF_GATE_A
    lr1 = lr0 + GLA_LOWRANK
    lr = jnp.pad(w[:, lr0:lr1], ((0, 0), (0, LANE - GLA_LOWRANK)))
    return w[:, :lr0].astype(BF16), w[:, lr1:].astype(BF16), lr.astype(BF16)


def _layer(x, p, g_mix, w_in, w_alpha, b_alpha, g_gla_out, w_out_gla, w_out_sb, w_out, g_ple, w_ple_gate,
           w_ple, g_last):
    bsz, seq, _ = x.shape
    t = bsz * seq
    x2 = x.reshape(t, D_MODEL)
    z, lr = _in_proj(x2, g_mix.reshape(1, D_MODEL), *_split_w_in(w_in))
    z3 = z.reshape(bsz, seq, Z_COLS)
    w_alpha_p = jnp.pad(w_alpha, ((0, LANE - GLA_LOWRANK), (0, 0))).astype(BF16)
    ga = _gla(z3, lr.reshape(bsz, seq, LANE), w_alpha_p, b_alpha.reshape(1, GLA_QK),
              g_gla_out.reshape(1, GLA_V))
    gb = _stick_breaking(z3)
    out = _out_proj(ga.reshape(t, GLA_V), gb.reshape(t, SB_W), z, x2, p.reshape(t, PLE_DIM),
                    w_out_gla.astype(BF16), w_out_sb.astype(BF16), w_out.astype(BF16),
                    w_ple_gate.astype(BF16), w_ple.astype(BF16),
                    g_ple.reshape(1, D_MODEL), g_last.reshape(1, D_MODEL))
    return out.reshape(bsz, seq, D_MODEL)


def kernel(x, p, g_mix, w_in, w_alpha, b_alpha, g_gla_out, w_out_gla, w_out_sb, w_out, g_ple, w_ple_gate, w_ple,
           g_final):
    depth = p.shape[0]
    assert depth == 1, "the fused final RMSNorm assumes a single layer"
    return _layer(x, p[0], g_mix[0], w_in[0], w_alpha[0], b_alpha[0], g_gla_out[0], w_out_gla[0], w_out_sb[0],
                  w_out[0], g_ple[0], w_ple_gate[0], w_ple[0], g_final)
```

```python
import jax
import jax.numpy as jnp
from jax import lax
from jax.experimental import pallas as pl
from jax.experimental.pallas import tpu as pltpu

D_MODEL = 1024
CHUNK = 64
PLE_DIM = 256
GLA_HEADS = 4
GLA_DK = 128
GLA_DV = 256
GLA_LOWRANK = 16
GLA_TAU = 16.0
SB_HEADS = 8
SB_DH = 128
GLA_QK = GLA_HEADS * GLA_DK
GLA_V = GLA_HEADS * GLA_DV
SB_W = SB_HEADS * SB_DH
EPS = 1e-6

LANE = 128

OFF_QK_A = 0
OFF_V_A = OFF_QK_A + 2 * GLA_QK
OFF_GATE_A = OFF_V_A + GLA_V
OFF_Q_B = OFF_GATE_A + GLA_V
OFF_K_B = OFF_Q_B + SB_W
OFF_V_B = OFF_K_B + SB_W
OFF_GATE_B = OFF_V_B + SB_W
OFF_M_A = OFF_GATE_B + SB_W
OFF_M_B = OFF_M_A + D_MODEL
Z_COLS = OFF_M_B + D_MODEL

LOG2_E = 1.4426950408889634
SB_DEAD_DEFICIT = 104.0 * LOG2_E
SB_NO_KEYS_DEFICIT = 1e30
SB_STREAMS = 32

W_SPLIT_ROWS = 128
IN_PROJ_ROWS = 512
IN_PROJ_COLS = 1024
SB_Q_BLOCK = 128
SB_K_STEP = 256
OUT_PROJ_ROWS = 1024
GLA_ROWS = 256
GLA_ATTN_ROWS = 2 * CHUNK

VMEM_LIMIT_BYTES = 56 * 1024 * 1024

BF16 = jnp.bfloat16
F32 = jnp.float32


def _dot(a, b):
    return jnp.dot(a, b, preferred_element_type=F32)


def _dot_nt(a, b):
    return lax.dot_general(a, b, (((1,), (1,)), ((), ())), preferred_element_type=F32)


def _dot_tn(a, b):
    return lax.dot_general(a, b, (((0,), (0,)), ((), ())), preferred_element_type=F32)


def _rms_scale(x):
    return lax.rsqrt(jnp.mean(x * x, axis=-1, keepdims=True) + EPS)


def _sigmoid(g):
    return 1.0 / (1.0 + jnp.exp(-g))


def _silu(g):
    return g * _sigmoid(g)


def _in_proj_kernel(x_ref, g_ref, w_head_ref, w_tail_ref, wlr_ref, z_ref, lr_ref):
    x = x_ref[...]
    h = (x * _rms_scale(x) * g_ref[...]).astype(BF16)
    for w_ref, z0 in ((w_head_ref, 0), (w_tail_ref, OFF_GATE_A)):
        width = w_ref.shape[1]
        for c in range(0, width, IN_PROJ_COLS):
            c1 = min(c + IN_PROJ_COLS, width)
            z_ref[:, z0 + c:z0 + c1] = _dot(h, w_ref[:, c:c1]).astype(BF16)
    lr_ref[...] = _dot(h, wlr_ref[...])


def _in_proj(x2, g_mix, w_head, w_tail, w_lr):
    t = x2.shape[0]
    return pl.pallas_call(
        _in_proj_kernel,
        grid=(t // IN_PROJ_ROWS,),
        in_specs=[
            pl.BlockSpec((IN_PROJ_ROWS, D_MODEL), lambda i: (i, 0)),
            pl.BlockSpec((1, D_MODEL), lambda i: (0, 0)),
            pl.BlockSpec((D_MODEL, OFF_GATE_A), lambda i: (0, 0), pipeline_mode=pl.Buffered(1)),
            pl.BlockSpec((D_MODEL, Z_COLS - OFF_GATE_A), lambda i: (0, 0), pipeline_mode=pl.Buffered(1)),
            pl.BlockSpec((D_MODEL, LANE), lambda i: (0, 0)),
        ],
        out_specs=[
            pl.BlockSpec((IN_PROJ_ROWS, Z_COLS), lambda i: (i, 0)),
            pl.BlockSpec((IN_PROJ_ROWS, LANE), lambda i: (i, 0)),
        ],
        out_shape=[
            jax.ShapeDtypeStruct((t, Z_COLS), BF16),
            jax.ShapeDtypeStruct((t, LANE), F32),
        ],
        compiler_params=pltpu.CompilerParams(
            dimension_semantics=("arbitrary",), vmem_limit_bytes=VMEM_LIMIT_BYTES),
        name="in_proj",
    )(x2, g_mix, w_head, w_tail, w_lr)


def _gla_kernel(qk_ref, v_ref, gate_ref, lr_ref, walpha_ref, balpha_ref, gout_ref, o_ref, state_ref):
    rows, arows = GLA_ROWS, GLA_ATTN_ROWS
    chunk_shift = CHUNK.bit_length() - 1

    @pl.when(pl.program_id(1) == 0)
    def _():
        state_ref[...] = jnp.zeros_like(state_ref)

    r_id = lax.broadcasted_iota(jnp.int32, (rows, rows), 0)
    c_id = lax.broadcasted_iota(jnp.int32, (rows, rows), 1)
    same_chunk = jnp.right_shift(r_id, chunk_shift) == jnp.right_shift(c_id, chunk_shift)
    cum_and_rest = jnp.concatenate([jnp.where(same_chunk & (r_id >= c_id), 1.0, 0.0),
                                    jnp.where(same_chunk & (r_id < c_id), 1.0, 0.0)], axis=0).astype(BF16)
    ar_id = lax.broadcasted_iota(jnp.int32, (arows, arows), 0)
    ac_id = lax.broadcasted_iota(jnp.int32, (arows, arows), 1)
    attn_same_chunk = jnp.right_shift(ar_id, chunk_shift) == jnp.right_shift(ac_id, chunk_shift)
    attn_causal = ar_id >= ac_id

    logits = _dot(lr_ref[...].astype(BF16), walpha_ref[...]) + balpha_ref[...]
    log_a = (jnp.minimum(logits, 0.0) - jnp.log(1.0 + jnp.exp(-jnp.abs(logits)))) * (1.0 / GLA_TAU)
    b_and_rest = _dot(cum_and_rest, log_a.astype(BF16))

    q_scale = GLA_DK ** -0.5
    for h in range(GLA_HEADS):
        ks = slice(h * GLA_DK, (h + 1) * GLA_DK)
        vs = slice(h * GLA_DV, (h + 1) * GLA_DV)
        q = qk_ref[:, h * GLA_DK:(h + 1) * GLA_DK]
        k = qk_ref[:, GLA_QK + h * GLA_DK:GLA_QK + (h + 1) * GLA_DK]
        v = v_ref[:, vs]
        b = b_and_rest[:rows, ks]
        rest = b_and_rest[rows:, ks]
        eb = jnp.exp(b)
        enb = jnp.exp(-b)
        qd = q * (eb * q_scale).astype(BF16)
        kd = k * enb.astype(BF16)
        qi = q * (enb * q_scale).astype(BF16)
        ki = k * eb.astype(BF16)
        k_end = k * jnp.exp(rest).astype(BF16)

        o_intra = []
        for a0 in range(0, rows, arows):
            sl = slice(a0, a0 + arows)
            a_past = _dot_nt(qd[sl], kd[sl])
            a_fut = _dot_nt(qi[sl], ki[sl])
            attn = jnp.where(attn_same_chunk, jnp.where(attn_causal, a_past, a_fut), 0.0)
            o_intra.append(_dot(attn.astype(BF16), v[sl]))

        state = state_ref[h]
        o_inter = []
        for c0 in range(0, rows, CHUNK):
            sl = slice(c0, c0 + CHUNK)
            o_inter.append(_dot_nt(qd[sl], state.astype(BF16)))
            decay = jnp.exp(b[c0:c0 + 1] + rest[c0:c0 + 1])
            state = state * decay + _dot_tn(v[sl], k_end[sl])
        state_ref[h] = state

        o = jnp.concatenate(o_intra, axis=0) + jnp.concatenate(o_inter, axis=0)
        o = o * _rms_scale(o) * gout_ref[:, vs]
        o_ref[:, vs] = (o * _silu(gate_ref[:, vs].astype(F32))).astype(BF16)


def _gla(z3, lr3, w_alpha, b_alpha, g_out):
    bsz, seq, _ = z3.shape
    nblk = D_MODEL
    return pl.pallas_call(
        _gla_kernel,
        grid=(bsz, seq // GLA_ROWS),
        in_specs=[
            pl.BlockSpec((None, GLA_ROWS, 2 * GLA_QK), lambda b, s: (b, s, OFF_QK_A // nblk)),
            pl.BlockSpec((None, GLA_ROWS, GLA_V), lambda b, s: (b, s, OFF_V_A // nblk)),
            pl.BlockSpec((None, GLA_ROWS, GLA_V), lambda b, s: (b, s, OFF_GATE_A // nblk)),
            pl.BlockSpec((None, GLA_ROWS, LANE), lambda b, s: (b, s, 0)),
            pl.BlockSpec((LANE, GLA_QK), lambda b, s: (0, 0)),
            pl.BlockSpec((1, GLA_QK), lambda b, s: (0, 0)),
            pl.BlockSpec((1, GLA_V), lambda b, s: (0, 0)),
        ],
        out_specs=pl.BlockSpec((None, GLA_ROWS, GLA_V), lambda b, s: (b, s, 0)),
        out_shape=jax.ShapeDtypeStruct((bsz, seq, GLA_V), BF16),
        scratch_shapes=[pltpu.VMEM((GLA_HEADS, GLA_DV, GLA_DK), F32)],
        compiler_params=pltpu.CompilerParams(
            dimension_semantics=("arbitrary", "arbitrary"), vmem_limit_bytes=VMEM_LIMIT_BYTES),
        name="gla",
    )(z3, z3, z3, lr3, w_alpha, b_alpha, g_out)


def _sb_kernel(q_ref, k_ref, v_ref, gate_ref, o_ref, acc_ref, spent_ref):
    seq = q_ref.shape[0]
    tq, kstep, n_streams = SB_Q_BLOCK, SB_K_STEP, SB_STREAMS
    to_log2_logit = (SB_DH ** -0.5) * LOG2_E

    def later_keys_matrix(n):
        r_id = lax.broadcasted_iota(jnp.int32, (n, n), 0)
        c_id = lax.broadcasted_iota(jnp.int32, (n, n), 1)
        return jnp.where(r_id > c_id, 1.0, 0.0).astype(BF16)

    later_keys, later_keys_diagonal = later_keys_matrix(kstep), later_keys_matrix(tq)
    strictly_before = lax.broadcasted_iota(jnp.int32, (tq, tq), 1) < lax.broadcasted_iota(jnp.int32, (tq, tq), 0)
    lane_id = lax.broadcasted_iota(jnp.int32, (tq, kstep), 1)

    def log2_terms(q0, k):
        u = _dot_nt(q_ref[pl.ds(q0, tq), :], k) * to_log2_logit
        deficit = jnp.maximum(u, 0.0) + jnp.log2(1.0 + 1.0 / jnp.exp2(jnp.abs(u)))
        return u - deficit, deficit

    def diagonal_step(i0):
        log2_betas, deficits = [], []
        for s in range(n_streams):
            q0 = pl.multiple_of((i0 + s) * tq, tq)
            log2_beta, deficit = log2_terms(q0, k_ref[pl.ds(q0, tq), :])
            log2_betas.append(log2_beta)
            deficits.append(jnp.where(strictly_before, deficit, 0.0))
        least_spent = None
        for s in range(n_streams):
            q0 = pl.multiple_of((i0 + s) * tq, tq)
            later = _dot(deficits[s].astype(BF16), later_keys_diagonal)
            w = jnp.where(strictly_before, jnp.exp2(log2_betas[s] - later), 0.0)
            acc_ref[s] = _dot(w.astype(BF16), v_ref[pl.ds(q0, tq), :])
            spent = jnp.broadcast_to(jnp.sum(deficits[s], axis=1, keepdims=True), (tq, tq))
            spent_ref[s] = spent
            least_spent = spent if least_spent is None else jnp.minimum(least_spent, spent)
        return jnp.min(least_spent)

    def key_step(i0, t, every_stream_has_keys):
        log2_betas, deficits, spents, vs, valids, k0s = [], [], [], [], [], []
        for s in range(n_streams):
            q0 = pl.multiple_of((i0 + s) * tq, tq)
            k0 = (i0 + s) * tq - t * kstep
            k0s.append(k0)
            if every_stream_has_keys:
                k0 = pl.multiple_of(k0, tq)
                k, v = k_ref[pl.ds(k0, kstep), :], v_ref[pl.ds(k0, kstep), :]
                spent, valid = spent_ref[s], None
            else:
                starts = [pl.multiple_of(jnp.maximum(k0 + off, 0), tq) for off in range(0, kstep, tq)]
                k = jnp.concatenate([k_ref[pl.ds(st, tq), :] for st in starts], axis=0)
                v = jnp.concatenate([v_ref[pl.ds(st, tq), :] for st in starts], axis=0)
                spent = jnp.where(k0 + kstep <= 0, SB_NO_KEYS_DEFICIT, spent_ref[s])
                valid = lane_id >= -k0
            log2_beta, deficit = log2_terms(q0, k)
            if valid is not None:
                deficit = jnp.where(valid, deficit, 0.0)
            log2_betas.append(log2_beta)
            deficits.append(deficit)
            spents.append(spent)
            vs.append(v)
            valids.append(valid)
        least_spent = None
        for s in range(n_streams):
            later = _dot(deficits[s].astype(BF16), later_keys)
            spent_wide = jnp.concatenate([spents[s]] * (kstep // tq), axis=1)
            w = jnp.exp2((log2_betas[s] - spent_wide) - later)
            if valids[s] is not None:
                w = jnp.where(valids[s], w, 0.0)
            acc_ref[s] += _dot(w.astype(BF16), vs[s])
            spent = spents[s] + jnp.sum(deficits[s], axis=1, keepdims=True)
            if valids[s] is not None:
                spent = jnp.where(k0s[s] <= 0, SB_NO_KEYS_DEFICIT, spent)
            spent_ref[s] = spent
            least_spent = spent if least_spent is None else jnp.minimum(least_spent, spent)
        return jnp.min(least_spent)

    def q_group(g, _):
        i0 = g * n_streams
        least_spent = diagonal_step(i0)

        def alive(st):
            return st[1] < SB_DEAD_DEFICIT

        def alive_and_every_stream_has_keys(st):
            return jnp.logical_and(alive(st), st[0] * kstep <= i0 * tq)

        def body_all_keys(st):
            return st[0] + 1, key_step(i0, st[0], True)

        def body(st):
            return st[0] + 1, key_step(i0, st[0], False)

        st = lax.while_loop(alive_and_every_stream_has_keys, body_all_keys, (jnp.int32(1), least_spent))
        lax.while_loop(alive, body, st)
        for s in range(n_streams):
            q0 = pl.multiple_of((i0 + s) * tq, tq)
            gate = gate_ref[pl.ds(q0, tq), :].astype(F32)
            o_ref[pl.ds(q0, tq), :] = (acc_ref[s] * _silu(gate)).astype(BF16)
        return 0

    lax.fori_loop(0, seq // (tq * n_streams), q_group, 0)


def _stick_breaking(z3):
    bsz, seq, _ = z3.shape

    def spec(off):
        return pl.BlockSpec((None, seq, SB_DH), lambda b, h: (b, 0, off // SB_DH + h))

    return pl.pallas_call(
        _sb_kernel,
        grid=(bsz, SB_HEADS),
        in_specs=[spec(OFF_Q_B), spec(OFF_K_B), spec(OFF_V_B), spec(OFF_GATE_B)],
        out_specs=pl.BlockSpec((None, seq, SB_DH), lambda b, h: (b, 0, h)),
        out_shape=jax.ShapeDtypeStruct((bsz, seq, SB_W), BF16),
        scratch_shapes=[pltpu.VMEM((SB_STREAMS, SB_Q_BLOCK, SB_DH), F32),
                        pltpu.VMEM((SB_STREAMS, SB_Q_BLOCK, SB_Q_BLOCK), F32)],
        compiler_params=pltpu.CompilerParams(
            dimension_semantics=("arbitrary", "arbitrary"), vmem_limit_bytes=VMEM_LIMIT_BYTES),
        name="stickbrk",
    )(z3, z3, z3, z3)


def _out_proj_kernel(ga_ref, gb_ref, ma_ref, mb_ref, x_ref, p_ref, wa_ref, wb_ref, wo_ref, wg_ref, wp_ref,
                     gple_ref, gfin_ref, o_ref):
    y_a = _dot(ga_ref[...], wa_ref[...])
    y_b = _dot(gb_ref[...], wb_ref[...])
    merged = _sigmoid(ma_ref[...].astype(F32)) * y_a + _sigmoid(mb_ref[...].astype(F32)) * y_b
    x1 = x_ref[...] + _dot(merged.astype(BF16), wo_ref[...])
    u = (x1 * _rms_scale(x1) * gple_ref[...]).astype(BF16)
    gate_p = _sigmoid(_dot(u, wg_ref[...]))
    x2 = x1 + gate_p * _dot(p_ref[...].astype(BF16), wp_ref[...])
    o_ref[...] = x2 * _rms_scale(x2) * gfin_ref[...]


def _out_proj(ga, gb, z2, x2, p2, wa, wb, wo, wg, wp, g_ple, g_final):
    t = x2.shape[0]
    rows = OUT_PROJ_ROWS

    def row_spec(width, col_block=0):
        return pl.BlockSpec((rows, width), lambda i: (i, col_block))

    def full_spec(shape):
        return pl.BlockSpec(shape, lambda i: (0, 0), pipeline_mode=pl.Buffered(1))

    return pl.pallas_call(
        _out_proj_kernel,
        grid=(t // rows,),
        in_specs=[
            row_spec(GLA_V), row_spec(SB_W),
            row_spec(D_MODEL, OFF_M_A // D_MODEL), row_spec(D_MODEL, OFF_M_B // D_MODEL),
            row_spec(D_MODEL), row_spec(PLE_DIM),
            full_spec((GLA_V, D_MODEL)), full_spec((SB_W, D_MODEL)), full_spec((D_MODEL, D_MODEL)),
            full_spec((D_MODEL, D_MODEL)), full_spec((PLE_DIM, D_MODEL)),
            full_spec((1, D_MODEL)), full_spec((1, D_MODEL)),
        ],
        out_specs=row_spec(D_MODEL),
        out_shape=jax.ShapeDtypeStruct((t, D_MODEL), F32),
        compiler_params=pltpu.CompilerParams(
            dimension_semantics=("arbitrary",), vmem_limit_bytes=VMEM_LIMIT_BYTES),
        name="out_proj",
    )(ga, gb, z2, z2, x2, p2, wa, wb, wo, wg, wp, g_ple, g_final)


def _split_w_in_kernel(w_ref, head_ref, tail_ref, lr_ref):
    lr0 = OFF_GATE_A
    lr1 = lr0 + GLA_LOWRANK
    head_ref[...] = w_ref[:, :lr0].astype(BF16)
    tail_ref[...] = w_ref[:, lr1:].astype(BF16)
    lr = w_ref[:, lr0:lr0 + LANE]
    keep = lax.broadcasted_iota(jnp.int32, lr.shape, 1) < GLA_LOWRANK
    lr_ref[...] = jnp.where(keep, lr, 0.0).astype(BF16)


def _split_w_in(w):
    d, d_in = w.shape
    rows = W_SPLIT_ROWS
    tail = d_in - OFF_GATE_A - GLA_LOWRANK
    return pl.pallas_call(
        _split_w_in_kernel,
        grid=(d // rows,),
        in_specs=[pl.BlockSpec((rows, d_in), lambda i: (i, 0))],
        out_specs=[pl.BlockSpec((rows, OFF_GATE_A), lambda i: (i, 0)),
                   pl.BlockSpec((rows, tail), lambda i: (i, 0)),
                   pl.BlockSpec((rows, LANE), lambda i: (i, 0))],
        out_shape=[jax.ShapeDtypeStruct((d, OFF_GATE_A), BF16),
                   jax.ShapeDtypeStruct((d, tail), BF16),
                   jax.ShapeDtypeStruct((d, LANE), BF16)],
        compiler_params=pltpu.CompilerParams(
            dimension_semantics=("arbitrary",), vmem_limit_bytes=VMEM_LIMIT_BYTES),
        name="split_w_in",
    )(w)


def _layer(x, p, g_mix, w_in, w_alpha, b_alpha, g_gla_out, w_out_gla, w_out_sb, w_out, g_ple, w_ple_gate,
           w_ple, g_last):
    bsz, seq, _ = x.shape
    t = bsz * seq
    x2 = x.reshape(t, D_MODEL)
    z, lr = _in_proj(x2, g_mix.reshape(1, D_MODEL), *_split_w_in(w_in))
    z3 = z.reshape(bsz, seq, Z_COLS)
    w_alpha_p = jnp.pad(w_alpha, ((0, LANE - GLA_LOWRANK), (0, 0))).astype(BF16)
    ga = _gla(z3, lr.reshape(bsz, seq, LANE), w_alpha_p, b_alpha.reshape(1, GLA_QK),
              g_gla_out.reshape(1, GLA_V))
    gb = _stick_breaking(z3)
    out = _out_proj(ga.reshape(t, GLA_V), gb.reshape(t, SB_W), z, x2, p.reshape(t, PLE_DIM),
                    w_out_gla.astype(BF16), w_out_sb.astype(BF16), w_out.astype(BF16),
                    w_ple_gate.astype(BF16), w_ple.astype(BF16),
                    g_ple.reshape(1, D_MODEL), g_last.reshape(1, D_MODEL))
    return out.reshape(bsz, seq, D_MODEL)


def kernel(x, p, g_mix, w_in, w_alpha, b_alpha, g_gla_out, w_out_gla, w_out_sb, w_out, g_ple, w_ple_gate, w_ple,
           g_final):
    depth = p.shape[0]
    assert depth == 1, "the fused final RMSNorm assumes a single layer"
    return _layer(x, p[0], g_mix[0], w_in[0], w_alpha[0], b_alpha[0], g_gla_out[0], w_out_gla[0], w_out_sb[0],
                  w_out[0], g_ple[0], w_ple_gate[0], w_ple[0], g_final)
```

```python
import jax
import jax.numpy as jnp
from jax import lax
from jax.experimental import pallas as pl
from jax.experimental.pallas import tpu as pltpu

D_MODEL = 1024
CHUNK = 64
PLE_DIM = 256
GLA_HEADS = 4
GLA_DK = 128
GLA_DV = 256
GLA_LOWRANK = 16
GLA_TAU = 16.0
SB_HEADS = 8
SB_DH = 128
GLA_QK = GLA_HEADS * GLA_DK
GLA_V = GLA_HEADS * GLA_DV
SB_W = SB_HEADS * SB_DH
EPS = 1e-6

LANE = 128

OFF_QK_A = 0
OFF_V_A = OFF_QK_A + 2 * GLA_QK
OFF_GATE_A = OFF_V_A + GLA_V
OFF_Q_B = OFF_GATE_A + GLA_V
OFF_K_B = OFF_Q_B + SB_W
OFF_V_B = OFF_K_B + SB_W
OFF_GATE_B = OFF_V_B + SB_W
OFF_M_A = OFF_GATE_B + SB_W
OFF_M_B = OFF_M_A + D_MODEL
Z_COLS = OFF_M_B + D_MODEL

LOG2_E = 1.4426950408889634
SB_DEAD_DEFICIT = 104.0 * LOG2_E
SB_NO_KEYS_DEFICIT = 1e30
SB_STREAMS = 32

IN_PROJ_ROWS = 512
IN_PROJ_COLS = 1024
SB_Q_BLOCK = 128
SB_K_STEP = 256
OUT_PROJ_ROWS = 1024
GLA_ROWS = 256
GLA_ATTN_ROWS = 2 * CHUNK

VMEM_LIMIT_BYTES = 56 * 1024 * 1024

BF16 = jnp.bfloat16
F32 = jnp.float32


def _dot(a, b):
    return jnp.dot(a, b, preferred_element_type=F32)


def _dot_nt(a, b):
    return lax.dot_general(a, b, (((1,), (1,)), ((), ())), preferred_element_type=F32)


def _dot_tn(a, b):
    return lax.dot_general(a, b, (((0,), (0,)), ((), ())), preferred_element_type=F32)


def _rms_scale(x):
    return lax.rsqrt(jnp.mean(x * x, axis=-1, keepdims=True) + EPS)


def _sigmoid(g):
    return 1.0 / (1.0 + jnp.exp(-g))


def _silu(g):
    return g * _sigmoid(g)


def _in_proj_kernel(x_ref, g_ref, wt_ref, z_ref, lr_ref):
    x = x_ref[...]
    h = (x * _rms_scale(x) * g_ref[...]).astype(BF16)
    lr0 = OFF_GATE_A
    lr1 = lr0 + GLA_LOWRANK
    for w0, z0, width in ((0, 0, lr0), (lr1, lr0, Z_COLS - lr0)):
        for c in range(0, width, IN_PROJ_COLS):
            c1 = min(c + IN_PROJ_COLS, width)
            z_ref[:, z0 + c:z0 + c1] = _dot_nt(h, wt_ref[w0 + c:w0 + c1, :]).astype(BF16)
    lr_ref[...] = _dot_nt(h, wt_ref[lr0:lr0 + LANE, :])


def _in_proj(x2, g_mix, w_in_t):
    t = x2.shape[0]
    return pl.pallas_call(
        _in_proj_kernel,
        grid=(t // IN_PROJ_ROWS,),
        in_specs=[
            pl.BlockSpec((IN_PROJ_ROWS, D_MODEL), lambda i: (i, 0)),
            pl.BlockSpec((1, D_MODEL), lambda i: (0, 0)),
            pl.BlockSpec(w_in_t.shape, lambda i: (0, 0), pipeline_mode=pl.Buffered(1)),
        ],
        out_specs=[
            pl.BlockSpec((IN_PROJ_ROWS, Z_COLS), lambda i: (i, 0)),
            pl.BlockSpec((IN_PROJ_ROWS, LANE), lambda i: (i, 0)),
        ],
        out_shape=[
            jax.ShapeDtypeStruct((t, Z_COLS), BF16),
            jax.ShapeDtypeStruct((t, LANE), F32),
        ],
        compiler_params=pltpu.CompilerParams(
            dimension_semantics=("arbitrary",), vmem_limit_bytes=VMEM_LIMIT_BYTES),
        name="in_proj",
    )(x2, g_mix, w_in_t)


def _gla_kernel(qk_ref, v_ref, gate_ref, lr_ref, walpha_ref, balpha_ref, gout_ref, o_ref, state_ref):
    rows, arows = GLA_ROWS, GLA_ATTN_ROWS
    chunk_shift = CHUNK.bit_length() - 1

    @pl.when(pl.program_id(1) == 0)
    def _():
        state_ref[...] = jnp.zeros_like(state_ref)

    r_id = lax.broadcasted_iota(jnp.int32, (rows, rows), 0)
    c_id = lax.broadcasted_iota(jnp.int32, (rows, rows), 1)
    same_chunk = jnp.right_shift(r_id, chunk_shift) == jnp.right_shift(c_id, chunk_shift)
    cum_and_rest = jnp.concatenate([jnp.where(same_chunk & (r_id >= c_id), 1.0, 0.0),
                                    jnp.where(same_chunk & (r_id < c_id), 1.0, 0.0)], axis=0).astype(BF16)
    ar_id = lax.broadcasted_iota(jnp.int32, (arows, arows), 0)
    ac_id = lax.broadcasted_iota(jnp.int32, (arows, arows), 1)
    attn_same_chunk = jnp.right_shift(ar_id, chunk_shift) == jnp.right_shift(ac_id, chunk_shift)
    attn_causal = ar_id >= ac_id

    logits = _dot(lr_ref[...].astype(BF16), walpha_ref[...]) + balpha_ref[...]
    log_a = (jnp.minimum(logits, 0.0) - jnp.log(1.0 + jnp.exp(-jnp.abs(logits)))) * (1.0 / GLA_TAU)
    b_and_rest = _dot(cum_and_rest, log_a.astype(BF16))

    q_scale = GLA_DK ** -0.5
    for h in range(GLA_HEADS):
        ks = slice(h * GLA_DK, (h + 1) * GLA_DK)
        vs = slice(h * GLA_DV, (h + 1) * GLA_DV)
        q = qk_ref[:, h * GLA_DK:(h + 1) * GLA_DK]
        k = qk_ref[:, GLA_QK + h * GLA_DK:GLA_QK + (h + 1) * GLA_DK]
        v = v_ref[:, vs]
        b = b_and_rest[:rows, ks]
        rest = b_and_rest[rows:, ks]
        eb = jnp.exp(b)
        enb = jnp.exp(-b)
        qd = q * (eb * q_scale).astype(BF16)
        kd = k * enb.astype(BF16)
        qi = q * (enb * q_scale).astype(BF16)
        ki = k * eb.astype(BF16)
        k_end = k * jnp.exp(rest).astype(BF16)

        o_intra = []
        for a0 in range(0, rows, arows):
            sl = slice(a0, a0 + arows)
            a_past = _dot_nt(qd[sl], kd[sl])
            a_fut = _dot_nt(qi[sl], ki[sl])
            attn = jnp.where(attn_same_chunk, jnp.where(attn_causal, a_past, a_fut), 0.0)
            o_intra.append(_dot(attn.astype(BF16), v[sl]))

        state = state_ref[h]
        o_inter = []
        for c0 in range(0, rows, CHUNK):
            sl = slice(c0, c0 + CHUNK)
            o_inter.append(_dot_nt(qd[sl], state.astype(BF16)))
            decay = jnp.exp(b[c0:c0 + 1] + rest[c0:c0 + 1])
            state = state * decay + _dot_tn(v[sl], k_end[sl])
        state_ref[h] = state

        o = jnp.concatenate(o_intra, axis=0) + jnp.concatenate(o_inter, axis=0)
        o = o * _rms_scale(o) * gout_ref[:, vs]
        o_ref[:, vs] = (o * _silu(gate_ref[:, vs].astype(F32))).astype(BF16)


def _gla(z3, lr3, w_alpha, b_alpha, g_out):
    bsz, seq, _ = z3.shape
    nblk = D_MODEL
    return pl.pallas_call(
        _gla_kernel,
        grid=(bsz, seq // GLA_ROWS),
        in_specs=[
            pl.BlockSpec((None, GLA_ROWS, 2 * GLA_QK), lambda b, s: (b, s, OFF_QK_A // nblk)),
            pl.BlockSpec((None, GLA_ROWS, GLA_V), lambda b, s: (b, s, OFF_V_A // nblk)),
            pl.BlockSpec((None, GLA_ROWS, GLA_V), lambda b, s: (b, s, OFF_GATE_A // nblk)),
            pl.BlockSpec((None, GLA_ROWS, LANE), lambda b, s: (b, s, 0)),
            pl.BlockSpec((LANE, GLA_QK), lambda b, s: (0, 0)),
            pl.BlockSpec((1, GLA_QK), lambda b, s: (0, 0)),
            pl.BlockSpec((1, GLA_V), lambda b, s: (0, 0)),
        ],
        out_specs=pl.BlockSpec((None, GLA_ROWS, GLA_V), lambda b, s: (b, s, 0)),
        out_shape=jax.ShapeDtypeStruct((bsz, seq, GLA_V), BF16),
        scratch_shapes=[pltpu.VMEM((GLA_HEADS, GLA_DV, GLA_DK), F32)],
        compiler_params=pltpu.CompilerParams(
            dimension_semantics=("arbitrary", "arbitrary"), vmem_limit_bytes=VMEM_LIMIT_BYTES),
        name="gla",
    )(z3, z3, z3, lr3, w_alpha, b_alpha, g_out)


def _sb_kernel(q_ref, k_ref, v_ref, gate_ref, o_ref, acc_ref, spent_ref):
    seq = q_ref.shape[0]
    tq, kstep, n_streams = SB_Q_BLOCK, SB_K_STEP, SB_STREAMS
    to_log2_logit = (SB_DH ** -0.5) * LOG2_E

    def later_keys_matrix(n):
        r_id = lax.broadcasted_iota(jnp.int32, (n, n), 0)
        c_id = lax.broadcasted_iota(jnp.int32, (n, n), 1)
        return jnp.where(r_id > c_id, 1.0, 0.0).astype(BF16)

    later_keys, later_keys_diagonal = later_keys_matrix(kstep), later_keys_matrix(tq)
    strictly_before = lax.broadcasted_iota(jnp.int32, (tq, tq), 1) < lax.broadcasted_iota(jnp.int32, (tq, tq), 0)
    lane_id = lax.broadcasted_iota(jnp.int32, (tq, kstep), 1)

    def log2_terms(q0, k):
        u = _dot_nt(q_ref[pl.ds(q0, tq), :], k) * to_log2_logit
        deficit = jnp.maximum(u, 0.0) + jnp.log2(1.0 + 1.0 / jnp.exp2(jnp.abs(u)))
        return u - deficit, deficit

    def diagonal_step(i0):
        log2_betas, deficits = [], []
        for s in range(n_streams):
            q0 = pl.multiple_of((i0 + s) * tq, tq)
            log2_beta, deficit = log2_terms(q0, k_ref[pl.ds(q0, tq), :])
            log2_betas.append(log2_beta)
            deficits.append(jnp.where(strictly_before, deficit, 0.0))
        least_spent = None
        for s in range(n_streams):
            q0 = pl.multiple_of((i0 + s) * tq, tq)
            later = _dot(deficits[s].astype(BF16), later_keys_diagonal)
            w = jnp.where(strictly_before, jnp.exp2(log2_betas[s] - later), 0.0)
            acc_ref[s] = _dot(w.astype(BF16), v_ref[pl.ds(q0, tq), :])
            spent = jnp.broadcast_to(jnp.sum(deficits[s], axis=1, keepdims=True), (tq, tq))
            spent_ref[s] = spent
            least_spent = spent if least_spent is None else jnp.minimum(least_spent, spent)
        return jnp.min(least_spent)

    def key_step(i0, t, every_stream_has_keys):
        log2_betas, deficits, spents, vs, valids, k0s = [], [], [], [], [], []
        for s in range(n_streams):
            q0 = pl.multiple_of((i0 + s) * tq, tq)
            k0 = (i0 + s) * tq - t * kstep
            k0s.append(k0)
            if every_stream_has_keys:
                k0 = pl.multiple_of(k0, tq)
                k, v = k_ref[pl.ds(k0, kstep), :], v_ref[pl.ds(k0, kstep), :]
                spent, valid = spent_ref[s], None
            else:
                starts = [pl.multiple_of(jnp.maximum(k0 + off, 0), tq) for off in range(0, kstep, tq)]
                k = jnp.concatenate([k_ref[pl.ds(st, tq), :] for st in starts], axis=0)
                v = jnp.concatenate([v_ref[pl.ds(st, tq), :] for st in starts], axis=0)
                spent = jnp.where(k0 + kstep <= 0, SB_NO_KEYS_DEFICIT, spent_ref[s])
                valid = lane_id >= -k0
            log2_beta, deficit = log2_terms(q0, k)
            if valid is not None:
                deficit = jnp.where(valid, deficit, 0.0)
            log2_betas.append(log2_beta)
            deficits.append(deficit)
            spents.append(spent)
            vs.append(v)
            valids.append(valid)
        least_spent = None
        for s in range(n_streams):
            later = _dot(deficits[s].astype(BF16), later_keys)
            spent_wide = jnp.concatenate([spents[s]] * (kstep // tq), axis=1)
            w = jnp.exp2((log2_betas[s] - spent_wide) - later)
            if valids[s] is not None:
                w = jnp.where(valids[s], w, 0.0)
            acc_ref[s] += _dot(w.astype(BF16), vs[s])
            spent = spents[s] + jnp.sum(deficits[s], axis=1, keepdims=True)
            if valids[s] is not None:
                spent = jnp.where(k0s[s] <= 0, SB_NO_KEYS_DEFICIT, spent)
            spent_ref[s] = spent
            least_spent = spent if least_spent is None else jnp.minimum(least_spent, spent)
        return jnp.min(least_spent)

    def q_group(g, _):
        i0 = g * n_streams
        least_spent = diagonal_step(i0)

        def alive(st):
            return st[1] < SB_DEAD_DEFICIT

        def alive_and_every_stream_has_keys(st):
            return jnp.logical_and(alive(st), st[0] * kstep <= i0 * tq)

        def body_all_keys(st):
            return st[0] + 1, key_step(i0, st[0], True)

        def body(st):
            return st[0] + 1, key_step(i0, st[0], False)

        st = lax.while_loop(alive_and_every_stream_has_keys, body_all_keys, (jnp.int32(1), least_spent))
        lax.while_loop(alive, body, st)
        for s in range(n_streams):
            q0 = pl.multiple_of((i0 + s) * tq, tq)
            gate = gate_ref[pl.ds(q0, tq), :].astype(F32)
            o_ref[pl.ds(q0, tq), :] = (acc_ref[s] * _silu(gate)).astype(BF16)
        return 0

    lax.fori_loop(0, seq // (tq * n_streams), q_group, 0)


def _stick_breaking(z3):
    bsz, seq, _ = z3.shape

    def spec(off):
        return pl.BlockSpec((None, seq, SB_DH), lambda b, h: (b, 0, off // SB_DH + h))

    return pl.pallas_call(
        _sb_kernel,
        grid=(bsz, SB_HEADS),
        in_specs=[spec(OFF_Q_B), spec(OFF_K_B), spec(OFF_V_B), spec(OFF_GATE_B)],
        out_specs=pl.BlockSpec((None, seq, SB_DH), lambda b, h: (b, 0, h)),
        out_shape=jax.ShapeDtypeStruct((bsz, seq, SB_W), BF16),
        scratch_shapes=[pltpu.VMEM((SB_STREAMS, SB_Q_BLOCK, SB_DH), F32),
                        pltpu.VMEM((SB_STREAMS, SB_Q_BLOCK, SB_Q_BLOCK), F32)],
        compiler_params=pltpu.CompilerParams(
            dimension_semantics=("arbitrary", "arbitrary"), vmem_limit_bytes=VMEM_LIMIT_BYTES),
        name="stickbrk",
    )(z3, z3, z3, z3)


def _out_proj_kernel(ga_ref, gb_ref, ma_ref, mb_ref, x_ref, p_ref, wa_ref, wb_ref, wo_ref, wg_ref, wp_ref,
                     gple_ref, gfin_ref, o_ref):
    y_a = _dot(ga_ref[...], wa_ref[...])
    y_b = _dot(gb_ref[...], wb_ref[...])
    merged = _sigmoid(ma_ref[...].astype(F32)) * y_a + _sigmoid(mb_ref[...].astype(F32)) * y_b
    x1 = x_ref[...] + _dot(merged.astype(BF16), wo_ref[...])
    u = (x1 * _rms_scale(x1) * gple_ref[...]).astype(BF16)
    gate_p = _sigmoid(_dot(u, wg_ref[...]))
    x2 = x1 + gate_p * _dot(p_ref[...].astype(BF16), wp_ref[...])
    o_ref[...] = x2 * _rms_scale(x2) * gfin_ref[...]


def _out_proj(ga, gb, z2, x2, p2, wa, wb, wo, wg, wp, g_ple, g_final):
    t = x2.shape[0]
    rows = OUT_PROJ_ROWS

    def row_spec(width, col_block=0):
        return pl.BlockSpec((rows, width), lambda i: (i, col_block))

    def full_spec(shape):
        return pl.BlockSpec(shape, lambda i: (0, 0), pipeline_mode=pl.Buffered(1))

    return pl.pallas_call(
        _out_proj_kernel,
        grid=(t // rows,),
        in_specs=[
            row_spec(GLA_V), row_spec(SB_W),
            row_spec(D_MODEL, OFF_M_A // D_MODEL), row_spec(D_MODEL, OFF_M_B // D_MODEL),
            row_spec(D_MODEL), row_spec(PLE_DIM),
            full_spec((GLA_V, D_MODEL)), full_spec((SB_W, D_MODEL)), full_spec((D_MODEL, D_MODEL)),
            full_spec((D_MODEL, D_MODEL)), full_spec((PLE_DIM, D_MODEL)),
            full_spec((1, D_MODEL)), full_spec((1, D_MODEL)),
        ],
        out_specs=row_spec(D_MODEL),
        out_shape=jax.ShapeDtypeStruct((t, D_MODEL), F32),
        compiler_params=pltpu.CompilerParams(
            dimension_semantics=("arbitrary",), vmem_limit_bytes=VMEM_LIMIT_BYTES),
        name="out_proj",
    )(ga, gb, z2, z2, x2, p2, wa, wb, wo, wg, wp, g_ple, g_final)


def _layer(layer, x, p, g_mix, w_in_stack, w_alpha, b_alpha, g_gla_out, w_out_gla, w_out_sb, w_out, g_ple,
           w_ple_gate, w_ple, g_last):
    bsz, seq, _ = x.shape
    t = bsz * seq
    x2 = x.reshape(t, D_MODEL)
    z, lr = _in_proj(x2, g_mix.reshape(1, D_MODEL), jnp.swapaxes(w_in_stack[layer], 0, 1).astype(BF16))
    z3 = z.reshape(bsz, seq, Z_COLS)
    w_alpha_p = jnp.pad(w_alpha, ((0, LANE - GLA_LOWRANK), (0, 0))).astype(BF16)
    ga = _gla(z3, lr.reshape(bsz, seq, LANE), w_alpha_p, b_alpha.reshape(1, GLA_QK),
              g_gla_out.reshape(1, GLA_V))
    gb = _stick_breaking(z3)
    out = _out_proj(ga.reshape(t, GLA_V), gb.reshape(t, SB_W), z, x2, p.reshape(t, PLE_DIM),
                    w_out_gla.astype(BF16), w_out_sb.astype(BF16), w_out.astype(BF16),
                    w_ple_gate.astype(BF16), w_ple.astype(BF16),
                    g_ple.reshape(1, D_MODEL), g_last.reshape(1, D_MODEL))
    return out.reshape(bsz, seq, D_MODEL)


def kernel(x, p, g_mix, w_in, w_alpha, b_alpha, g_gla_out, w_out_gla, w_out_sb, w_out, g_ple, w_ple_gate, w_ple,
           g_final):
    depth = p.shape[0]
    assert depth == 1, "the fused final RMSNorm assumes a single layer"
    return _layer(0, x, p[0], g_mix[0], w_in, w_alpha[0], b_alpha[0], g_gla_out[0], w_out_gla[0], w_out_sb[0],
                  w_out[0], g_ple[0], w_ple_gate[0], w_ple[0], g_final)
```

```python
import jax
import jax.numpy as jnp
from jax import lax
from jax.experimental import pallas as pl
from jax.experimental.pallas import tpu as pltpu

D_MODEL = 1024
CHUNK = 64
PLE_DIM = 256
GLA_HEADS = 4
GLA_DK = 128
GLA_DV = 256
GLA_LOWRANK = 16
GLA_TAU = 16.0
SB_HEADS = 8
SB_DH = 128
GLA_QK = GLA_HEADS * GLA_DK
GLA_V = GLA_HEADS * GLA_DV
SB_W = SB_HEADS * SB_DH
EPS = 1e-6

LANE = 128

OFF_QK_A = 0
OFF_V_A = OFF_QK_A + 2 * GLA_QK
OFF_GATE_A = OFF_V_A + GLA_V
OFF_Q_B = OFF_GATE_A + GLA_V
OFF_K_B = OFF_Q_B + SB_W
OFF_V_B = OFF_K_B + SB_W
OFF_GATE_B = OFF_V_B + SB_W
OFF_M_A = OFF_GATE_B + SB_W
OFF_M_B = OFF_M_A + D_MODEL
Z_COLS = OFF_M_B + D_MODEL

LOG2_E = 1.4426950408889634
SB_DEAD_DEFICIT = 104.0 * LOG2_E
SB_NO_KEYS_DEFICIT = 1e30
SB_STREAMS = 64

IN_PROJ_ROWS = 512
IN_PROJ_COLS = 1024
SB_Q_BLOCK = 64
SB_K_STEP = 256
OUT_PROJ_ROWS = 1024
GLA_ROWS = 256
GLA_ATTN_ROWS = 2 * CHUNK

VMEM_LIMIT_BYTES = 56 * 1024 * 1024

BF16 = jnp.bfloat16
F32 = jnp.float32


def _dot(a, b):
    return jnp.dot(a, b, preferred_element_type=F32)


def _dot_nt(a, b):
    return lax.dot_general(a, b, (((1,), (1,)), ((), ())), preferred_element_type=F32)


def _dot_tn(a, b):
    return lax.dot_general(a, b, (((0,), (0,)), ((), ())), preferred_element_type=F32)


def _rms_scale(x):
    return lax.rsqrt(jnp.mean(x * x, axis=-1, keepdims=True) + EPS)


def _sigmoid(g):
    return 1.0 / (1.0 + jnp.exp(-g))


def _silu(g):
    return g * _sigmoid(g)


def _in_proj_kernel(x_ref, g_ref, wt_ref, z_ref, lr_ref):
    x = x_ref[...]
    h = (x * _rms_scale(x) * g_ref[...]).astype(BF16)
    lr0 = OFF_GATE_A
    lr1 = lr0 + GLA_LOWRANK
    for w0, z0, width in ((0, 0, lr0), (lr1, lr0, Z_COLS - lr0)):
        for c in range(0, width, IN_PROJ_COLS):
            c1 = min(c + IN_PROJ_COLS, width)
            z_ref[:, z0 + c:z0 + c1] = _dot_nt(h, wt_ref[w0 + c:w0 + c1, :]).astype(BF16)
    lr_ref[...] = _dot_nt(h, wt_ref[lr0:lr0 + LANE, :])


def _in_proj(x2, g_mix, w_in_t):
    t = x2.shape[0]
    return pl.pallas_call(
        _in_proj_kernel,
        grid=(t // IN_PROJ_ROWS,),
        in_specs=[
            pl.BlockSpec((IN_PROJ_ROWS, D_MODEL), lambda i: (i, 0)),
            pl.BlockSpec((1, D_MODEL), lambda i: (0, 0)),
            pl.BlockSpec(w_in_t.shape, lambda i: (0, 0), pipeline_mode=pl.Buffered(1)),
        ],
        out_specs=[
            pl.BlockSpec((IN_PROJ_ROWS, Z_COLS), lambda i: (i, 0)),
            pl.BlockSpec((IN_PROJ_ROWS, LANE), lambda i: (i, 0)),
        ],
        out_shape=[
            jax.ShapeDtypeStruct((t, Z_COLS), BF16),
            jax.ShapeDtypeStruct((t, LANE), F32),
        ],
        compiler_params=pltpu.CompilerParams(
            dimension_semantics=("arbitrary",), vmem_limit_bytes=VMEM_LIMIT_BYTES),
        name="in_proj",
    )(x2, g_mix, w_in_t)


def _gla_kernel(qk_ref, v_ref, gate_ref, lr_ref, walpha_ref, balpha_ref, gout_ref, o_ref, state_ref):
    rows, arows = GLA_ROWS, GLA_ATTN_ROWS
    chunk_shift = CHUNK.bit_length() - 1

    @pl.when(pl.program_id(1) == 0)
    def _():
        state_ref[...] = jnp.zeros_like(state_ref)

    r_id = lax.broadcasted_iota(jnp.int32, (rows, rows), 0)
    c_id = lax.broadcasted_iota(jnp.int32, (rows, rows), 1)
    same_chunk = jnp.right_shift(r_id, chunk_shift) == jnp.right_shift(c_id, chunk_shift)
    cum_and_rest = jnp.concatenate([jnp.where(same_chunk & (r_id >= c_id), 1.0, 0.0),
                                    jnp.where(same_chunk & (r_id < c_id), 1.0, 0.0)], axis=0).astype(BF16)
    ar_id = lax.broadcasted_iota(jnp.int32, (arows, arows), 0)
    ac_id = lax.broadcasted_iota(jnp.int32, (arows, arows), 1)
    attn_same_chunk = jnp.right_shift(ar_id, chunk_shift) == jnp.right_shift(ac_id, chunk_shift)
    attn_causal = ar_id >= ac_id

    logits = _dot(lr_ref[...].astype(BF16), walpha_ref[...]) + balpha_ref[...]
    log_a = (jnp.minimum(logits, 0.0) - jnp.log(1.0 + jnp.exp(-jnp.abs(logits)))) * (1.0 / GLA_TAU)
    b_and_rest = _dot(cum_and_rest, log_a.astype(BF16))

    q_scale = GLA_DK ** -0.5
    for h in range(GLA_HEADS):
        ks = slice(h * GLA_DK, (h + 1) * GLA_DK)
        vs = slice(h * GLA_DV, (h + 1) * GLA_DV)
        q = qk_ref[:, h * GLA_DK:(h + 1) * GLA_DK]
        k = qk_ref[:, GLA_QK + h * GLA_DK:GLA_QK + (h + 1) * GLA_DK]
        v = v_ref[:, vs]
        b = b_and_rest[:rows, ks]
        rest = b_and_rest[rows:, ks]
        eb = jnp.exp(b)
        enb = jnp.exp(-b)
        qd = q * (eb * q_scale).astype(BF16)
        kd = k * enb.astype(BF16)
        qi = q * (enb * q_scale).astype(BF16)
        ki = k * eb.astype(BF16)
        k_end = k * jnp.exp(rest).astype(BF16)

        o_intra = []
        for a0 in range(0, rows, arows):
            sl = slice(a0, a0 + arows)
            a_past = _dot_nt(qd[sl], kd[sl])
            a_fut = _dot_nt(qi[sl], ki[sl])
            attn = jnp.where(attn_same_chunk, jnp.where(attn_causal, a_past, a_fut), 0.0)
            o_intra.append(_dot(attn.astype(BF16), v[sl]))

        state = state_ref[h]
        o_inter = []
        for c0 in range(0, rows, CHUNK):
            sl = slice(c0, c0 + CHUNK)
            o_inter.append(_dot_nt(qd[sl], state.astype(BF16)))
            decay = jnp.exp(b[c0:c0 + 1] + rest[c0:c0 + 1])
            state = state * decay + _dot_tn(v[sl], k_end[sl])
        state_ref[h] = state

        o = jnp.concatenate(o_intra, axis=0) + jnp.concatenate(o_inter, axis=0)
        o = o * _rms_scale(o) * gout_ref[:, vs]
        o_ref[:, vs] = (o * _silu(gate_ref[:, vs].astype(F32))).astype(BF16)


def _gla(z3, lr3, w_alpha, b_alpha, g_out):
    bsz, seq, _ = z3.shape
    nblk = D_MODEL
    return pl.pallas_call(
        _gla_kernel,
        grid=(bsz, seq // GLA_ROWS),
        in_specs=[
            pl.BlockSpec((None, GLA_ROWS, 2 * GLA_QK), lambda b, s: (b, s, OFF_QK_A // nblk)),
            pl.BlockSpec((None, GLA_ROWS, GLA_V), lambda b, s: (b, s, OFF_V_A // nblk)),
            pl.BlockSpec((None, GLA_ROWS, GLA_V), lambda b, s: (b, s, OFF_GATE_A // nblk)),
            pl.BlockSpec((None, GLA_ROWS, LANE), lambda b, s: (b, s, 0)),
            pl.BlockSpec((LANE, GLA_QK), lambda b, s: (0, 0)),
            pl.BlockSpec((1, GLA_QK), lambda b, s: (0, 0)),
            pl.BlockSpec((1, GLA_V), lambda b, s: (0, 0)),
        ],
        out_specs=pl.BlockSpec((None, GLA_ROWS, GLA_V), lambda b, s: (b, s, 0)),
        out_shape=jax.ShapeDtypeStruct((bsz, seq, GLA_V), BF16),
        scratch_shapes=[pltpu.VMEM((GLA_HEADS, GLA_DV, GLA_DK), F32)],
        compiler_params=pltpu.CompilerParams(
            dimension_semantics=("arbitrary", "arbitrary"), vmem_limit_bytes=VMEM_LIMIT_BYTES),
        name="gla",
    )(z3, z3, z3, lr3, w_alpha, b_alpha, g_out)


def _sb_kernel(q_ref, k_ref, v_ref, gate_ref, o_ref, acc_ref, spent_ref):
    seq = q_ref.shape[0]
    tq, kstep, n_streams = SB_Q_BLOCK, SB_K_STEP, SB_STREAMS
    lookback = kstep - tq
    to_log2_logit = (SB_DH ** -0.5) * LOG2_E
    r_id = lax.broadcasted_iota(jnp.int32, (kstep, kstep), 0)
    c_id = lax.broadcasted_iota(jnp.int32, (kstep, kstep), 1)
    later_keys = jnp.where(r_id > c_id, 1.0, 0.0).astype(BF16)
    lane_id = lax.broadcasted_iota(jnp.int32, (tq, kstep), 1)
    strictly_before = lane_id < lax.broadcasted_iota(jnp.int32, (tq, kstep), 0) + lookback

    def window(i0, t, first, every_stream_has_keys):
        log2_betas, deficits, spents, vs, masks, k0s = [], [], [], [], [], []
        for s in range(n_streams):
            q0 = pl.multiple_of((i0 + s) * tq, tq)
            k0 = (i0 + s) * tq - lookback - t * kstep
            k0s.append(k0)
            if every_stream_has_keys:
                k0 = pl.multiple_of(k0, tq)
                k, v = k_ref[pl.ds(k0, kstep), :], v_ref[pl.ds(k0, kstep), :]
                mask = strictly_before if first else None
                spent = None if first else spent_ref[s]
            else:
                starts = [pl.multiple_of(jnp.maximum(k0 + off, 0), tq) for off in range(0, kstep, tq)]
                k = jnp.concatenate([k_ref[pl.ds(st, tq), :] for st in starts], axis=0)
                v = jnp.concatenate([v_ref[pl.ds(st, tq), :] for st in starts], axis=0)
                mask = lane_id >= -k0
                if first:
                    mask = jnp.logical_and(mask, strictly_before)
                spent = None if first else jnp.where(k0 + kstep <= 0, SB_NO_KEYS_DEFICIT, spent_ref[s])
            u = _dot_nt(q_ref[pl.ds(q0, tq), :], k) * to_log2_logit
            deficit = jnp.maximum(u, 0.0) + jnp.log2(1.0 + 1.0 / jnp.exp2(jnp.abs(u)))
            log2_betas.append(u - deficit)
            deficits.append(deficit if mask is None else jnp.where(mask, deficit, 0.0))
            spents.append(spent)
            vs.append(v)
            masks.append(mask)
        laters = []
        for s in range(0, n_streams, 2):
            pair = _dot(jnp.concatenate([deficits[s], deficits[s + 1]], axis=0).astype(BF16), later_keys)
            laters += [pair[:tq], pair[tq:]]
        least_spent = None
        for s in range(n_streams):
            exponent = log2_betas[s] - laters[s]
            if not first:
                exponent = exponent - jnp.concatenate([spents[s]] * (kstep // LANE), axis=1)
            w = jnp.exp2(exponent)
            if masks[s] is not None:
                w = jnp.where(masks[s], w, 0.0)
            pv = _dot(w.astype(BF16), vs[s])
            total = jnp.sum(deficits[s], axis=1, keepdims=True)
            if first:
                acc_ref[s] = pv
                spent = jnp.broadcast_to(total, (tq, LANE))
            else:
                acc_ref[s] += pv
                spent = spents[s] + total
            if not every_stream_has_keys:
                spent = jnp.where(k0s[s] <= 0, SB_NO_KEYS_DEFICIT, spent)
            spent_ref[s] = spent
            least_spent = spent if least_spent is None else jnp.minimum(least_spent, spent)
        return jnp.min(least_spent)

    def q_group(g, _):
        i0 = g * n_streams
        least_spent = lax.cond(i0 * tq >= lookback,
                               lambda: window(i0, 0, True, True),
                               lambda: window(i0, 0, True, False))

        def alive(st):
            return st[1] < SB_DEAD_DEFICIT

        def body(st):
            return st[0] + 1, window(i0, st[0], False, False)

        lax.while_loop(alive, body, (jnp.int32(1), least_spent))
        for s in range(n_streams):
            q0 = pl.multiple_of((i0 + s) * tq, tq)
            gate = gate_ref[pl.ds(q0, tq), :].astype(F32)
            o_ref[pl.ds(q0, tq), :] = (acc_ref[s] * _silu(gate)).astype(BF16)
        return 0

    lax.fori_loop(0, seq // (tq * n_streams), q_group, 0)


def _stick_breaking(z3):
    bsz, seq, _ = z3.shape

    def spec(off):
        return pl.BlockSpec((None, seq, SB_DH), lambda b, h: (b, 0, off // SB_DH + h))

    return pl.pallas_call(
        _sb_kernel,
        grid=(bsz, SB_HEADS),
        in_specs=[spec(OFF_Q_B), spec(OFF_K_B), spec(OFF_V_B), spec(OFF_GATE_B)],
        out_specs=pl.BlockSpec((None, seq, SB_DH), lambda b, h: (b, 0, h)),
        out_shape=jax.ShapeDtypeStruct((bsz, seq, SB_W), BF16),
        scratch_shapes=[pltpu.VMEM((SB_STREAMS, SB_Q_BLOCK, SB_DH), F32),
                        pltpu.VMEM((SB_STREAMS, SB_Q_BLOCK, LANE), F32)],
        compiler_params=pltpu.CompilerParams(
            dimension_semantics=("arbitrary", "arbitrary"), vmem_limit_bytes=VMEM_LIMIT_BYTES),
        name="stickbrk",
    )(z3, z3, z3, z3)


def _out_proj_kernel(ga_ref, gb_ref, ma_ref, mb_ref, x_ref, p_ref, wa_ref, wb_ref, wo_ref, wg_ref, wp_ref,
                     gple_ref, gfin_ref, o_ref):
    y_a = _dot(ga_ref[...], wa_ref[...])
    y_b = _dot(gb_ref[...], wb_ref[...])
    merged = _sigmoid(ma_ref[...].astype(F32)) * y_a + _sigmoid(mb_ref[...].astype(F32)) * y_b
    x1 = x_ref[...] + _dot(merged.astype(BF16), wo_ref[...])
    u = (x1 * _rms_scale(x1) * gple_ref[...]).astype(BF16)
    gate_p = _sigmoid(_dot(u, wg_ref[...]))
    x2 = x1 + gate_p * _dot(p_ref[...].astype(BF16), wp_ref[...])
    o_ref[...] = x2 * _rms_scale(x2) * gfin_ref[...]


def _out_proj(ga, gb, z2, x2, p2, wa, wb, wo, wg, wp, g_ple, g_final):
    t = x2.shape[0]
    rows = OUT_PROJ_ROWS

    def row_spec(width, col_block=0):
        return pl.BlockSpec((rows, width), lambda i: (i, col_block))

    def full_spec(shape):
        return pl.BlockSpec(shape, lambda i: (0, 0), pipeline_mode=pl.Buffered(1))

    return pl.pallas_call(
        _out_proj_kernel,
        grid=(t // rows,),
        in_specs=[
            row_spec(GLA_V), row_spec(SB_W),
            row_spec(D_MODEL, OFF_M_A // D_MODEL), row_spec(D_MODEL, OFF_M_B // D_MODEL),
            row_spec(D_MODEL), row_spec(PLE_DIM),
            full_spec((GLA_V, D_MODEL)), full_spec((SB_W, D_MODEL)), full_spec((D_MODEL, D_MODEL)),
            full_spec((D_MODEL, D_MODEL)), full_spec((PLE_DIM, D_MODEL)),
            full_spec((1, D_MODEL)), full_spec((1, D_MODEL)),
        ],
        out_specs=row_spec(D_MODEL),
        out_shape=jax.ShapeDtypeStruct((t, D_MODEL), F32),
        compiler_params=pltpu.CompilerParams(
            dimension_semantics=("arbitrary",), vmem_limit_bytes=VMEM_LIMIT_BYTES),
        name="out_proj",
    )(ga, gb, z2, z2, x2, p2, wa, wb, wo, wg, wp, g_ple, g_final)


def _layer(layer, x, p, g_mix, w_in_stack, w_alpha, b_alpha, g_gla_out, w_out_gla, w_out_sb, w_out, g_ple,
           w_ple_gate, w_ple, g_last):
    bsz, seq, _ = x.shape
    t = bsz * seq
    x2 = x.reshape(t, D_MODEL)
    z, lr = _in_proj(x2, g_mix.reshape(1, D_MODEL), jnp.swapaxes(w_in_stack[layer], 0, 1).astype(BF16))
    z3 = z.reshape(bsz, seq, Z_COLS)
    w_alpha_p = jnp.pad(w_alpha, ((0, LANE - GLA_LOWRANK), (0, 0))).astype(BF16)
    ga = _gla(z3, lr.reshape(bsz, seq, LANE), w_alpha_p, b_alpha.reshape(1, GLA_QK),
              g_gla_out.reshape(1, GLA_V))
    gb = _stick_breaking(z3)
    out = _out_proj(ga.reshape(t, GLA_V), gb.reshape(t, SB_W), z, x2, p.reshape(t, PLE_DIM),
                    w_out_gla.astype(BF16), w_out_sb.astype(BF16), w_out.astype(BF16),
                    w_ple_gate.astype(BF16), w_ple.astype(BF16),
                    g_ple.reshape(1, D_MODEL), g_last.reshape(1, D_MODEL))
    return out.reshape(bsz, seq, D_MODEL)


def kernel(x, p, g_mix, w_in, w_alpha, b_alpha, g_gla_out, w_out_gla, w_out_sb, w_out, g_ple, w_ple_gate, w_ple,
           g_final):
    depth = p.shape[0]
    assert depth == 1, "the fused final RMSNorm assumes a single layer"
    return _layer(0, x, p[0], g_mix[0], w_in, w_alpha[0], b_alpha[0], g_gla_out[0], w_out_gla[0], w_out_sb[0],
                  w_out[0], g_ple[0], w_ple_gate[0], w_ple[0], g_final)
```

```python
import jax
import jax.numpy as jnp
from jax import lax
from jax.experimental import pallas as pl
from jax.experimental.pallas import tpu as pltpu

D_MODEL = 1024
CHUNK = 64
PLE_DIM = 256
GLA_HEADS = 4
GLA_DK = 128
GLA_DV = 256
GLA_LOWRANK = 16
GLA_TAU = 16.0
SB_HEADS = 8
SB_DH = 128
GLA_QK = GLA_HEADS * GLA_DK
GLA_V = GLA_HEADS * GLA_DV
SB_W = SB_HEADS * SB_DH
EPS = 1e-6

LANE = 128

OFF_QK_A = 0
OFF_V_A = OFF_QK_A + 2 * GLA_QK
OFF_GATE_A = OFF_V_A + GLA_V
OFF_Q_B = OFF_GATE_A + GLA_V
OFF_K_B = OFF_Q_B + SB_W
OFF_V_B = OFF_K_B + SB_W
OFF_GATE_B = OFF_V_B + SB_W
OFF_M_A = OFF_GATE_B + SB_W
OFF_M_B = OFF_M_A + D_MODEL
Z_COLS = OFF_M_B + D_MODEL

LOG2_E = 1.4426950408889634
SB_DEAD_DEFICIT = 104.0 * LOG2_E
SB_NO_KEYS_DEFICIT = 1e30
SB_STREAMS = 64

IN_PROJ_ROWS = 512
IN_PROJ_COLS = 1024
SB_Q_BLOCK = 64
SB_K_STEP = 256
OUT_PROJ_ROWS = 1024
GLA_ROWS = 256
GLA_ATTN_ROWS = 2 * CHUNK

VMEM_LIMIT_BYTES = 56 * 1024 * 1024

BF16 = jnp.bfloat16
F32 = jnp.float32


def _dot(a, b):
    return jnp.dot(a, b, preferred_element_type=F32)


def _dot_nt(a, b):
    return lax.dot_general(a, b, (((1,), (1,)), ((), ())), preferred_element_type=F32)


def _dot_tn(a, b):
    return lax.dot_general(a, b, (((0,), (0,)), ((), ())), preferred_element_type=F32)


def _rms_scale(x):
    return lax.rsqrt(jnp.mean(x * x, axis=-1, keepdims=True) + EPS)


def _sigmoid(g):
    return 1.0 / (1.0 + jnp.exp(-g))


def _silu(g):
    return g * _sigmoid(g)


def _in_proj_kernel(x_ref, g_ref, wt_ref, z_ref, lr_ref):
    x = x_ref[...]
    h = (x * _rms_scale(x) * g_ref[...]).astype(BF16)
    lr0 = OFF_GATE_A
    lr1 = lr0 + GLA_LOWRANK
    for w0, z0, width in ((0, 0, lr0), (lr1, lr0, Z_COLS - lr0)):
        for c in range(0, width, IN_PROJ_COLS):
            c1 = min(c + IN_PROJ_COLS, width)
            z_ref[:, z0 + c:z0 + c1] = _dot_nt(h, wt_ref[w0 + c:w0 + c1, :]).astype(BF16)
    lr_ref[...] = _dot_nt(h, wt_ref[lr0:lr0 + LANE, :])


def _in_proj(x2, g_mix, w_in_t):
    t = x2.shape[0]
    return pl.pallas_call(
        _in_proj_kernel,
        grid=(t // IN_PROJ_ROWS,),
        in_specs=[
            pl.BlockSpec((IN_PROJ_ROWS, D_MODEL), lambda i: (i, 0)),
            pl.BlockSpec((1, D_MODEL), lambda i: (0, 0)),
            pl.BlockSpec(w_in_t.shape, lambda i: (0, 0), pipeline_mode=pl.Buffered(1)),
        ],
        out_specs=[
            pl.BlockSpec((IN_PROJ_ROWS, Z_COLS), lambda i: (i, 0)),
            pl.BlockSpec((IN_PROJ_ROWS, LANE), lambda i: (i, 0)),
        ],
        out_shape=[
            jax.ShapeDtypeStruct((t, Z_COLS), BF16),
            jax.ShapeDtypeStruct((t, LANE), F32),
        ],
        compiler_params=pltpu.CompilerParams(
            dimension_semantics=("arbitrary",), vmem_limit_bytes=VMEM_LIMIT_BYTES),
        name="in_proj",
    )(x2, g_mix, w_in_t)


def _gla_kernel(qk_ref, v_ref, gate_ref, lr_ref, walpha_ref, balpha_ref, gout_ref, o_ref, state_ref):
    rows, arows = GLA_ROWS, GLA_ATTN_ROWS
    chunk_shift = CHUNK.bit_length() - 1

    @pl.when(pl.program_id(1) == 0)
    def _():
        state_ref[...] = jnp.zeros_like(state_ref)

    r_id = lax.broadcasted_iota(jnp.int32, (rows, rows), 0)
    c_id = lax.broadcasted_iota(jnp.int32, (rows, rows), 1)
    same_chunk = jnp.right_shift(r_id, chunk_shift) == jnp.right_shift(c_id, chunk_shift)
    cum_and_rest = jnp.concatenate([jnp.where(same_chunk & (r_id >= c_id), 1.0, 0.0),
                                    jnp.where(same_chunk & (r_id < c_id), 1.0, 0.0)], axis=0).astype(BF16)
    ar_id = lax.broadcasted_iota(jnp.int32, (arows, arows), 0)
    ac_id = lax.broadcasted_iota(jnp.int32, (arows, arows), 1)
    attn_same_chunk = jnp.right_shift(ar_id, chunk_shift) == jnp.right_shift(ac_id, chunk_shift)
    attn_causal = ar_id >= ac_id

    logits = _dot(lr_ref[...].astype(BF16), walpha_ref[...]) + balpha_ref[...]
    log_a = (jnp.minimum(logits, 0.0) - jnp.log(1.0 + jnp.exp(-jnp.abs(logits)))) * (1.0 / GLA_TAU)
    b_and_rest = _dot(cum_and_rest, log_a.astype(BF16))

    q_scale = GLA_DK ** -0.5
    for h in range(GLA_HEADS):
        ks = slice(h * GLA_DK, (h + 1) * GLA_DK)
        vs = slice(h * GLA_DV, (h + 1) * GLA_DV)
        q = qk_ref[:, h * GLA_DK:(h + 1) * GLA_DK]
        k = qk_ref[:, GLA_QK + h * GLA_DK:GLA_QK + (h + 1) * GLA_DK]
        v = v_ref[:, vs]
        b = b_and_rest[:rows, ks]
        rest = b_and_rest[rows:, ks]
        eb = jnp.exp(b)
        enb = jnp.exp(-b)
        qd = q * (eb * q_scale).astype(BF16)
        kd = k * enb.astype(BF16)
        qi = q * (enb * q_scale).astype(BF16)
        ki = k * eb.astype(BF16)
        k_end = k * jnp.exp(rest).astype(BF16)

        o_intra = []
        for a0 in range(0, rows, arows):
            sl = slice(a0, a0 + arows)
            a_past = _dot_nt(qd[sl], kd[sl])
            a_fut = _dot_nt(qi[sl], ki[sl])
            attn = jnp.where(attn_same_chunk, jnp.where(attn_causal, a_past, a_fut), 0.0)
            o_intra.append(_dot(attn.astype(BF16), v[sl]))

        n_chunks = rows // CHUNK
        in_first_chunk = lax.broadcasted_iota(jnp.int32, (2 * CHUNK, GLA_DK), 0) < CHUNK
        updates = []
        for c in range(0, n_chunks, 2):
            sl = slice(c * CHUNK, (c + 2) * CHUNK)
            ke = k_end[sl]
            zero = jnp.zeros_like(ke)
            both = _dot_tn(v[sl], jnp.concatenate([jnp.where(in_first_chunk, ke, zero),
                                                   jnp.where(in_first_chunk, zero, ke)], axis=1))
            updates += [both[:, :GLA_DK], both[:, GLA_DK:]]
        state = state_ref[h]
        states = []
        for c in range(n_chunks):
            c0 = c * CHUNK
            states.append(state.astype(BF16))
            decay = jnp.exp(b[c0:c0 + 1] + rest[c0:c0 + 1])
            state = state * decay + updates[c]
        state_ref[h] = state
        o_inter = []
        for c in range(0, n_chunks, 2):
            both = _dot_nt(qd[c * CHUNK:(c + 2) * CHUNK], jnp.concatenate(states[c:c + 2], axis=0))
            o_inter += [both[:CHUNK, :GLA_DV], both[CHUNK:, GLA_DV:]]

        o = jnp.concatenate(o_intra, axis=0) + jnp.concatenate(o_inter, axis=0)
        o = o * _rms_scale(o) * gout_ref[:, vs]
        o_ref[:, vs] = (o * _silu(gate_ref[:, vs].astype(F32))).astype(BF16)


def _gla(z3, lr3, w_alpha, b_alpha, g_out):
    bsz, seq, _ = z3.shape
    nblk = D_MODEL
    return pl.pallas_call(
        _gla_kernel,
        grid=(bsz, seq // GLA_ROWS),
        in_specs=[
            pl.BlockSpec((None, GLA_ROWS, 2 * GLA_QK), lambda b, s: (b, s, OFF_QK_A // nblk)),
            pl.BlockSpec((None, GLA_ROWS, GLA_V), lambda b, s: (b, s, OFF_V_A // nblk)),
            pl.BlockSpec((None, GLA_ROWS, GLA_V), lambda b, s: (b, s, OFF_GATE_A // nblk)),
            pl.BlockSpec((None, GLA_ROWS, LANE), lambda b, s: (b, s, 0)),
            pl.BlockSpec((LANE, GLA_QK), lambda b, s: (0, 0)),
            pl.BlockSpec((1, GLA_QK), lambda b, s: (0, 0)),
            pl.BlockSpec((1, GLA_V), lambda b, s: (0, 0)),
        ],
        out_specs=pl.BlockSpec((None, GLA_ROWS, GLA_V), lambda b, s: (b, s, 0)),
        out_shape=jax.ShapeDtypeStruct((bsz, seq, GLA_V), BF16),
        scratch_shapes=[pltpu.VMEM((GLA_HEADS, GLA_DV, GLA_DK), F32)],
        compiler_params=pltpu.CompilerParams(
            dimension_semantics=("arbitrary", "arbitrary"), vmem_limit_bytes=VMEM_LIMIT_BYTES),
        name="gla",
    )(z3, z3, z3, lr3, w_alpha, b_alpha, g_out)


def _sb_kernel(q_ref, k_ref, v_ref, gate_ref, o_ref, acc_ref, spent_ref):
    seq = q_ref.shape[0]
    tq, kstep, n_streams = SB_Q_BLOCK, SB_K_STEP, SB_STREAMS
    lookback = kstep - tq
    to_log2_logit = (SB_DH ** -0.5) * LOG2_E
    r_id = lax.broadcasted_iota(jnp.int32, (kstep, kstep), 0)
    c_id = lax.broadcasted_iota(jnp.int32, (kstep, kstep), 1)
    later_keys = jnp.where(r_id > c_id, 1.0, 0.0).astype(BF16)
    lane_id = lax.broadcasted_iota(jnp.int32, (tq, kstep), 1)
    strictly_before = lane_id < lax.broadcasted_iota(jnp.int32, (tq, kstep), 0) + lookback

    def window(i0, t, first, every_stream_has_keys):
        log2_betas, deficits, spents, vs, masks, k0s = [], [], [], [], [], []
        for s in range(n_streams):
            q0 = pl.multiple_of((i0 + s) * tq, tq)
            k0 = (i0 + s) * tq - lookback - t * kstep
            k0s.append(k0)
            if every_stream_has_keys:
                k0 = pl.multiple_of(k0, tq)
                k, v = k_ref[pl.ds(k0, kstep), :], v_ref[pl.ds(k0, kstep), :]
                mask = strictly_before if first else None
                spent = None if first else spent_ref[s]
            else:
                starts = [pl.multiple_of(jnp.maximum(k0 + off, 0), tq) for off in range(0, kstep, tq)]
                k = jnp.concatenate([k_ref[pl.ds(st, tq), :] for st in starts], axis=0)
                v = jnp.concatenate([v_ref[pl.ds(st, tq), :] for st in starts], axis=0)
                mask = lane_id >= -k0
                if first:
                    mask = jnp.logical_and(mask, strictly_before)
                spent = None if first else jnp.where(k0 + kstep <= 0, SB_NO_KEYS_DEFICIT, spent_ref[s])
            u = _dot_nt(q_ref[pl.ds(q0, tq), :], k) * to_log2_logit
            deficit = jnp.maximum(u, 0.0) + jnp.log2(1.0 + 1.0 / jnp.exp2(jnp.abs(u)))
            log2_betas.append(u - deficit)
            deficits.append(deficit if mask is None else jnp.where(mask, deficit, 0.0))
            spents.append(spent)
            vs.append(v)
            masks.append(mask)
        laters = []
        for s in range(0, n_streams, 2):
            pair = _dot(jnp.concatenate([deficits[s], deficits[s + 1]], axis=0).astype(BF16), later_keys)
            laters += [pair[:tq], pair[tq:]]
        least_spent = None
        for s in range(n_streams):
            exponent = log2_betas[s] - laters[s]
            if not first:
                exponent = exponent - jnp.concatenate([spents[s]] * (kstep // LANE), axis=1)
            w = jnp.exp2(exponent)
            if masks[s] is not None:
                w = jnp.where(masks[s], w, 0.0)
            pv = _dot(w.astype(BF16), vs[s])
            total = jnp.sum(deficits[s], axis=1, keepdims=True)
            if first:
                acc_ref[s] = pv
                spent = jnp.broadcast_to(total, (tq, LANE))
            else:
                acc_ref[s] += pv
                spent = spents[s] + total
            if not every_stream_has_keys:
                spent = jnp.where(k0s[s] <= 0, SB_NO_KEYS_DEFICIT, spent)
            spent_ref[s] = spent
            least_spent = spent if least_spent is None else jnp.minimum(least_spent, spent)
        return jnp.min(least_spent)

    def q_group(g, _):
        i0 = g * n_streams
        least_spent = lax.cond(i0 * tq >= lookback,
                               lambda: window(i0, 0, True, True),
                               lambda: window(i0, 0, True, False))

        def alive(st):
            return st[1] < SB_DEAD_DEFICIT

        def body(st):
            return st[0] + 1, window(i0, st[0], False, False)

        lax.while_loop(alive, body, (jnp.int32(1), least_spent))
        for s in range(n_streams):
            q0 = pl.multiple_of((i0 + s) * tq, tq)
            gate = gate_ref[pl.ds(q0, tq), :].astype(F32)
            o_ref[pl.ds(q0, tq), :] = (acc_ref[s] * _silu(gate)).astype(BF16)
        return 0

    lax.fori_loop(0, seq // (tq * n_streams), q_group, 0)


def _stick_breaking(z3):
    bsz, seq, _ = z3.shape

    def spec(off):
        return pl.BlockSpec((None, seq, SB_DH), lambda b, h: (b, 0, off // SB_DH + h))

    return pl.pallas_call(
        _sb_kernel,
        grid=(bsz, SB_HEADS),
        in_specs=[spec(OFF_Q_B), spec(OFF_K_B), spec(OFF_V_B), spec(OFF_GATE_B)],
        out_specs=pl.BlockSpec((None, seq, SB_DH), lambda b, h: (b, 0, h)),
        out_shape=jax.ShapeDtypeStruct((bsz, seq, SB_W), BF16),
        scratch_shapes=[pltpu.VMEM((SB_STREAMS, SB_Q_BLOCK, SB_DH), F32),
                        pltpu.VMEM((SB_STREAMS, SB_Q_BLOCK, LANE), F32)],
        compiler_params=pltpu.CompilerParams(
            dimension_semantics=("arbitrary", "arbitrary"), vmem_limit_bytes=VMEM_LIMIT_BYTES),
        name="stickbrk",
    )(z3, z3, z3, z3)


def _out_proj_kernel(ga_ref, gb_ref, ma_ref, mb_ref, x_ref, p_ref, wa_ref, wb_ref, wo_ref, wg_ref, wp_ref,
                     gple_ref, gfin_ref, o_ref):
    y_a = _dot(ga_ref[...], wa_ref[...])
    y_b = _dot(gb_ref[...], wb_ref[...])
    merged = _sigmoid(ma_ref[...].astype(F32)) * y_a + _sigmoid(mb_ref[...].astype(F32)) * y_b
    x1 = x_ref[...] + _dot(merged.astype(BF16), wo_ref[...])
    u = (x1 * _rms_scale(x1) * gple_ref[...]).astype(BF16)
    gate_p = _sigmoid(_dot(u, wg_ref[...]))
    x2 = x1 + gate_p * _dot(p_ref[...].astype(BF16), wp_ref[...])
    o_ref[...] = x2 * _rms_scale(x2) * gfin_ref[...]


def _out_proj(ga, gb, z2, x2, p2, wa, wb, wo, wg, wp, g_ple, g_final):
    t = x2.shape[0]
    rows = OUT_PROJ_ROWS

    def row_spec(width, col_block=0):
        return pl.BlockSpec((rows, width), lambda i: (i, col_block))

    def full_spec(shape):
        return pl.BlockSpec(shape, lambda i: (0, 0), pipeline_mode=pl.Buffered(1))

    return pl.pallas_call(
        _out_proj_kernel,
        grid=(t // rows,),
        in_specs=[
            row_spec(GLA_V), row_spec(SB_W),
            row_spec(D_MODEL, OFF_M_A // D_MODEL), row_spec(D_MODEL, OFF_M_B // D_MODEL),
            row_spec(D_MODEL), row_spec(PLE_DIM),
            full_spec((GLA_V, D_MODEL)), full_spec((SB_W, D_MODEL)), full_spec((D_MODEL, D_MODEL)),
            full_spec((D_MODEL, D_MODEL)), full_spec((PLE_DIM, D_MODEL)),
            full_spec((1, D_MODEL)), full_spec((1, D_MODEL)),
        ],
        out_specs=row_spec(D_MODEL),
        out_shape=jax.ShapeDtypeStruct((t, D_MODEL), F32),
        compiler_params=pltpu.CompilerParams(
            dimension_semantics=("arbitrary",), vmem_limit_bytes=VMEM_LIMIT_BYTES),
        name="out_proj",
    )(ga, gb, z2, z2, x2, p2, wa, wb, wo, wg, wp, g_ple, g_final)


def _layer(layer, x, p, g_mix, w_in_stack, w_alpha, b_alpha, g_gla_out, w_out_gla, w_out_sb, w_out, g_ple,
           w_ple_gate, w_ple, g_last):
    bsz, seq, _ = x.shape
    t = bsz * seq
    x2 = x.reshape(t, D_MODEL)
    z, lr = _in_proj(x2, g_mix.reshape(1, D_MODEL), jnp.swapaxes(w_in_stack[layer], 0, 1).astype(BF16))
    z3 = z.reshape(bsz, seq, Z_COLS)
    w_alpha_p = jnp.pad(w_alpha, ((0, LANE - GLA_LOWRANK), (0, 0))).astype(BF16)
    ga = _gla(z3, lr.reshape(bsz, seq, LANE), w_alpha_p, b_alpha.reshape(1, GLA_QK),
              g_gla_out.reshape(1, GLA_V))
    gb = _stick_breaking(z3)
    out = _out_proj(ga.reshape(t, GLA_V), gb.reshape(t, SB_W), z, x2, p.reshape(t, PLE_DIM),
                    w_out_gla.astype(BF16), w_out_sb.astype(BF16), w_out.astype(BF16),
                    w_ple_gate.astype(BF16), w_ple.astype(BF16),
                    g_ple.reshape(1, D_MODEL), g_last.reshape(1, D_MODEL))
    return out.reshape(bsz, seq, D_MODEL)


def kernel(x, p, g_mix, w_in, w_alpha, b_alpha, g_gla_out, w_out_gla, w_out_sb, w_out, g_ple, w_ple_gate, w_ple,
           g_final):
    depth = p.shape[0]
    assert depth == 1, "the fused final RMSNorm assumes a single layer"
    return _layer(0, x, p[0], g_mix[0], w_in, w_alpha[0], b_alpha[0], g_gla_out[0], w_out_gla[0], w_out_sb[0],
                  w_out[0], g_ple[0], w_ple_gate[0], w_ple[0], g_final)
```

```python
import jax
import jax.numpy as jnp
from jax import lax
from jax.experimental import pallas as pl
from jax.experimental.pallas import tpu as pltpu

D_MODEL = 1024
CHUNK = 64
PLE_DIM = 256
GLA_HEADS = 4
GLA_DK = 128
GLA_DV = 256
GLA_LOWRANK = 16
GLA_TAU = 16.0
SB_HEADS = 8
SB_DH = 128
GLA_QK = GLA_HEADS * GLA_DK
GLA_V = GLA_HEADS * GLA_DV
SB_W = SB_HEADS * SB_DH
EPS = 1e-6

LANE = 128

OFF_QK_A = 0
OFF_V_A = OFF_QK_A + 2 * GLA_QK
OFF_GATE_A = OFF_V_A + GLA_V
OFF_Q_B = OFF_GATE_A + GLA_V
OFF_K_B = OFF_Q_B + SB_W
OFF_V_B = OFF_K_B + SB_W
OFF_GATE_B = OFF_V_B + SB_W
OFF_M_A = OFF_GATE_B + SB_W
OFF_M_B = OFF_M_A + D_MODEL
Z_COLS = OFF_M_B + D_MODEL

LOG2_E = 1.4426950408889634
SB_DEAD_DEFICIT = 104.0 * LOG2_E
SB_NO_KEYS_DEFICIT = 1e30
SB_STREAMS = 64

IN_PROJ_ROWS = 512
IN_PROJ_COLS = 1024
SB_Q_BLOCK = 64
SB_K_STEP = 256
OUT_PROJ_ROWS = 1024
GLA_ROWS = 256
GLA_ATTN_ROWS = 4 * CHUNK

VMEM_LIMIT_BYTES = 56 * 1024 * 1024

BF16 = jnp.bfloat16
F32 = jnp.float32


def _dot(a, b):
    return jnp.dot(a, b, preferred_element_type=F32)


def _dot_nt(a, b):
    return lax.dot_general(a, b, (((1,), (1,)), ((), ())), preferred_element_type=F32)


def _dot_tn(a, b):
    return lax.dot_general(a, b, (((0,), (0,)), ((), ())), preferred_element_type=F32)


def _rms_scale(x):
    return lax.rsqrt(jnp.mean(x * x, axis=-1, keepdims=True) + EPS)


def _sigmoid(g):
    return 1.0 / (1.0 + jnp.exp(-g))


def _silu(g):
    return g * _sigmoid(g)


def _in_proj_kernel(x_ref, g_ref, wt_ref, z_ref, lr_ref):
    x = x_ref[...]
    h = (x * _rms_scale(x) * g_ref[...]).astype(BF16)
    lr0 = OFF_GATE_A
    lr1 = lr0 + GLA_LOWRANK
    for w0, z0, width in ((0, 0, lr0), (lr1, lr0, Z_COLS - lr0)):
        for c in range(0, width, IN_PROJ_COLS):
            c1 = min(c + IN_PROJ_COLS, width)
            z_ref[:, z0 + c:z0 + c1] = _dot_nt(h, wt_ref[w0 + c:w0 + c1, :]).astype(BF16)
    lr_ref[...] = _dot_nt(h, wt_ref[lr0:lr0 + LANE, :])


def _in_proj(x2, g_mix, w_in_t):
    t = x2.shape[0]
    return pl.pallas_call(
        _in_proj_kernel,
        grid=(t // IN_PROJ_ROWS,),
        in_specs=[
            pl.BlockSpec((IN_PROJ_ROWS, D_MODEL), lambda i: (i, 0)),
            pl.BlockSpec((1, D_MODEL), lambda i: (0, 0)),
            pl.BlockSpec(w_in_t.shape, lambda i: (0, 0), pipeline_mode=pl.Buffered(1)),
        ],
        out_specs=[
            pl.BlockSpec((IN_PROJ_ROWS, Z_COLS), lambda i: (i, 0)),
            pl.BlockSpec((IN_PROJ_ROWS, LANE), lambda i: (i, 0)),
        ],
        out_shape=[
            jax.ShapeDtypeStruct((t, Z_COLS), BF16),
            jax.ShapeDtypeStruct((t, LANE), F32),
        ],
        compiler_params=pltpu.CompilerParams(
            dimension_semantics=("arbitrary",), vmem_limit_bytes=VMEM_LIMIT_BYTES),
        name="in_proj",
    )(x2, g_mix, w_in_t)


def _gla_kernel(qk_ref, v_ref, gate_ref, lr_ref, walpha_ref, balpha_ref, gout_ref, o_ref, state_ref):
    rows, arows = GLA_ROWS, GLA_ATTN_ROWS
    chunk_shift = CHUNK.bit_length() - 1

    @pl.when(pl.program_id(1) == 0)
    def _():
        state_ref[...] = jnp.zeros_like(state_ref)

    r_id = lax.broadcasted_iota(jnp.int32, (rows, rows), 0)
    c_id = lax.broadcasted_iota(jnp.int32, (rows, rows), 1)
    same_chunk = jnp.right_shift(r_id, chunk_shift) == jnp.right_shift(c_id, chunk_shift)
    cum_and_rest = jnp.concatenate([jnp.where(same_chunk & (r_id >= c_id), 1.0, 0.0),
                                    jnp.where(same_chunk & (r_id < c_id), 1.0, 0.0)], axis=0).astype(BF16)
    ar_id = lax.broadcasted_iota(jnp.int32, (arows, arows), 0)
    ac_id = lax.broadcasted_iota(jnp.int32, (arows, arows), 1)
    attn_same_chunk = jnp.right_shift(ar_id, chunk_shift) == jnp.right_shift(ac_id, chunk_shift)
    attn_causal = ar_id >= ac_id

    logits = _dot(lr_ref[...].astype(BF16), walpha_ref[...]) + balpha_ref[...]
    log_a = (jnp.minimum(logits, 0.0) - jnp.log(1.0 + jnp.exp(-jnp.abs(logits)))) * (1.0 / GLA_TAU)
    b_and_rest = _dot(cum_and_rest, log_a.astype(BF16))

    q_scale = GLA_DK ** -0.5
    for h in range(GLA_HEADS):
        ks = slice(h * GLA_DK, (h + 1) * GLA_DK)
        vs = slice(h * GLA_DV, (h + 1) * GLA_DV)
        q = qk_ref[:, h * GLA_DK:(h + 1) * GLA_DK]
        k = qk_ref[:, GLA_QK + h * GLA_DK:GLA_QK + (h + 1) * GLA_DK]
        v = v_ref[:, vs]
        b = b_and_rest[:rows, ks]
        rest = b_and_rest[rows:, ks]
        eb = jnp.exp(b)
        enb = jnp.exp(-b)
        qd = q * (eb * q_scale).astype(BF16)
        kd = k * enb.astype(BF16)
        qi = q * (enb * q_scale).astype(BF16)
        ki = k * eb.astype(BF16)
        k_end = k * jnp.exp(rest).astype(BF16)

        o_intra = []
        for a0 in range(0, rows, arows):
            sl = slice(a0, a0 + arows)
            a_past = _dot_nt(qd[sl], kd[sl])
            a_fut = _dot_nt(qi[sl], ki[sl])
            attn = jnp.where(attn_same_chunk, jnp.where(attn_causal, a_past, a_fut), 0.0)
            o_intra.append(_dot(attn.astype(BF16), v[sl]))

        n_chunks = rows // CHUNK
        in_first_chunk = lax.broadcasted_iota(jnp.int32, (2 * CHUNK, GLA_DK), 0) < CHUNK
        updates = []
        for c in range(0, n_chunks, 2):
            sl = slice(c * CHUNK, (c + 2) * CHUNK)
            ke = k_end[sl]
            zero = jnp.zeros_like(ke)
            both = _dot_tn(v[sl], jnp.concatenate([jnp.where(in_first_chunk, ke, zero),
                                                   jnp.where(in_first_chunk, zero, ke)], axis=1))
            updates += [both[:, :GLA_DK], both[:, GLA_DK:]]
        state = state_ref[h]
        states = []
        for c in range(n_chunks):
            c0 = c * CHUNK
            states.append(state.astype(BF16))
            decay = jnp.exp(b[c0:c0 + 1] + rest[c0:c0 + 1])
            state = state * decay + updates[c]
        state_ref[h] = state
        o_inter = []
        for c in range(0, n_chunks, 2):
            both = _dot_nt(qd[c * CHUNK:(c + 2) * CHUNK], jnp.concatenate(states[c:c + 2], axis=0))
            o_inter += [both[:CHUNK, :GLA_DV], both[CHUNK:, GLA_DV:]]

        o = jnp.concatenate(o_intra, axis=0) + jnp.concatenate(o_inter, axis=0)
        o = o * _rms_scale(o) * gout_ref[:, vs]
        o_ref[:, vs] = (o * _silu(gate_ref[:, vs].astype(F32))).astype(BF16)


def _gla(z3, lr3, w_alpha, b_alpha, g_out):
    bsz, seq, _ = z3.shape
    nblk = D_MODEL
    return pl.pallas_call(
        _gla_kernel,
        grid=(bsz, seq // GLA_ROWS),
        in_specs=[
            pl.BlockSpec((None, GLA_ROWS, 2 * GLA_QK), lambda b, s: (b, s, OFF_QK_A // nblk)),
            pl.BlockSpec((None, GLA_ROWS, GLA_V), lambda b, s: (b, s, OFF_V_A // nblk)),
            pl.BlockSpec((None, GLA_ROWS, GLA_V), lambda b, s: (b, s, OFF_GATE_A // nblk)),
            pl.BlockSpec((None, GLA_ROWS, LANE), lambda b, s: (b, s, 0)),
            pl.BlockSpec((LANE, GLA_QK), lambda b, s: (0, 0)),
            pl.BlockSpec((1, GLA_QK), lambda b, s: (0, 0)),
            pl.BlockSpec((1, GLA_V), lambda b, s: (0, 0)),
        ],
        out_specs=pl.BlockSpec((None, GLA_ROWS, GLA_V), lambda b, s: (b, s, 0)),
        out_shape=jax.ShapeDtypeStruct((bsz, seq, GLA_V), BF16),
        scratch_shapes=[pltpu.VMEM((GLA_HEADS, GLA_DV, GLA_DK), F32)],
        compiler_params=pltpu.CompilerParams(
            dimension_semantics=("arbitrary", "arbitrary"), vmem_limit_bytes=VMEM_LIMIT_BYTES),
        name="gla",
    )(z3, z3, z3, lr3, w_alpha, b_alpha, g_out)


def _sb_kernel(q_ref, k_ref, v_ref, gate_ref, o_ref, acc_ref, spent_ref):
    seq = q_ref.shape[0]
    tq, kstep, n_streams = SB_Q_BLOCK, SB_K_STEP, SB_STREAMS
    lookback = kstep - tq
    to_log2_logit = (SB_DH ** -0.5) * LOG2_E
    r_id = lax.broadcasted_iota(jnp.int32, (kstep, kstep), 0)
    c_id = lax.broadcasted_iota(jnp.int32, (kstep, kstep), 1)
    later_keys = jnp.where(r_id > c_id, 1.0, 0.0).astype(BF16)
    lane_id = lax.broadcasted_iota(jnp.int32, (tq, kstep), 1)
    strictly_before = lane_id < lax.broadcasted_iota(jnp.int32, (tq, kstep), 0) + lookback

    def window(i0, t, first, every_stream_has_keys):
        log2_betas, deficits, spents, vs, masks, k0s = [], [], [], [], [], []
        for s in range(n_streams):
            q0 = pl.multiple_of((i0 + s) * tq, tq)
            k0 = (i0 + s) * tq - lookback - t * kstep
            k0s.append(k0)
            if every_stream_has_keys:
                k0 = pl.multiple_of(k0, tq)
                k, v = k_ref[pl.ds(k0, kstep), :], v_ref[pl.ds(k0, kstep), :]
                mask = strictly_before if first else None
                spent = None if first else spent_ref[s]
            else:
                starts = [pl.multiple_of(jnp.maximum(k0 + off, 0), tq) for off in range(0, kstep, tq)]
                k = jnp.concatenate([k_ref[pl.ds(st, tq), :] for st in starts], axis=0)
                v = jnp.concatenate([v_ref[pl.ds(st, tq), :] for st in starts], axis=0)
                mask = lane_id >= -k0
                if first:
                    mask = jnp.logical_and(mask, strictly_before)
                spent = None if first else jnp.where(k0 + kstep <= 0, SB_NO_KEYS_DEFICIT, spent_ref[s])
            u = _dot_nt(q_ref[pl.ds(q0, tq), :], k) * to_log2_logit
            deficit = jnp.maximum(u, 0.0) + jnp.log2(1.0 + 1.0 / jnp.exp2(jnp.abs(u)))
            log2_betas.append(u - deficit)
            deficits.append(deficit if mask is None else jnp.where(mask, deficit, 0.0))
            spents.append(spent)
            vs.append(v)
            masks.append(mask)
        laters = []
        for s in range(0, n_streams, 2):
            pair = _dot(jnp.concatenate([deficits[s], deficits[s + 1]], axis=0).astype(BF16), later_keys)
            laters += [pair[:tq], pair[tq:]]
        least_spent = None
        for s in range(n_streams):
            exponent = log2_betas[s] - laters[s]
            if not first:
                exponent = exponent - jnp.concatenate([spents[s]] * (kstep // LANE), axis=1)
            w = jnp.exp2(exponent)
            if masks[s] is not None:
                w = jnp.where(masks[s], w, 0.0)
            pv = _dot(w.astype(BF16), vs[s])
            total = jnp.sum(deficits[s], axis=1, keepdims=True)
            if first:
                acc_ref[s] = pv
                spent = jnp.broadcast_to(total, (tq, LANE))
            else:
                acc_ref[s] += pv
                spent = spents[s] + total
            if not every_stream_has_keys:
                spent = jnp.where(k0s[s] <= 0, SB_NO_KEYS_DEFICIT, spent)
            spent_ref[s] = spent
            least_spent = spent if least_spent is None else jnp.minimum(least_spent, spent)
        return jnp.min(least_spent)

    def q_group(g, _):
        i0 = g * n_streams
        least_spent = lax.cond(i0 * tq >= lookback,
                               lambda: window(i0, 0, True, True),
                               lambda: window(i0, 0, True, False))

        def alive(st):
            return st[1] < SB_DEAD_DEFICIT

        def body(st):
            return st[0] + 1, window(i0, st[0], False, False)

        lax.while_loop(alive, body, (jnp.int32(1), least_spent))
        for s in range(n_streams):
            q0 = pl.multiple_of((i0 + s) * tq, tq)
            gate = gate_ref[pl.ds(q0, tq), :].astype(F32)
            o_ref[pl.ds(q0, tq), :] = (acc_ref[s] * _silu(gate)).astype(BF16)
        return 0

    lax.fori_loop(0, seq // (tq * n_streams), q_group, 0)


def _stick_breaking(z3):
    bsz, seq, _ = z3.shape

    def spec(off):
        return pl.BlockSpec((None, seq, SB_DH), lambda b, h: (b, 0, off // SB_DH + h))

    return pl.pallas_call(
        _sb_kernel,
        grid=(bsz, SB_HEADS),
        in_specs=[spec(OFF_Q_B), spec(OFF_K_B), spec(OFF_V_B), spec(OFF_GATE_B)],
        out_specs=pl.BlockSpec((None, seq, SB_DH), lambda b, h: (b, 0, h)),
        out_shape=jax.ShapeDtypeStruct((bsz, seq, SB_W), BF16),
        scratch_shapes=[pltpu.VMEM((SB_STREAMS, SB_Q_BLOCK, SB_DH), F32),
                        pltpu.VMEM((SB_STREAMS, SB_Q_BLOCK, LANE), F32)],
        compiler_params=pltpu.CompilerParams(
            dimension_semantics=("arbitrary", "arbitrary"), vmem_limit_bytes=VMEM_LIMIT_BYTES),
        name="stickbrk",
    )(z3, z3, z3, z3)


def _out_proj_kernel(ga_ref, gb_ref, ma_ref, mb_ref, x_ref, p_ref, wa_ref, wb_ref, wo_ref, wg_ref, wp_ref,
                     gple_ref, gfin_ref, o_ref):
    y_a = _dot(ga_ref[...], wa_ref[...])
    y_b = _dot(gb_ref[...], wb_ref[...])
    merged = _sigmoid(ma_ref[...].astype(F32)) * y_a + _sigmoid(mb_ref[...].astype(F32)) * y_b
    x1 = x_ref[...] + _dot(merged.astype(BF16), wo_ref[...])
    u = (x1 * _rms_scale(x1) * gple_ref[...]).astype(BF16)
    gate_p = _sigmoid(_dot(u, wg_ref[...]))
    x2 = x1 + gate_p * _dot(p_ref[...].astype(BF16), wp_ref[...])
    o_ref[...] = x2 * _rms_scale(x2) * gfin_ref[...]


def _out_proj(ga, gb, z2, x2, p2, wa, wb, wo, wg, wp, g_ple, g_final):
    t = x2.shape[0]
    rows = OUT_PROJ_ROWS

    def row_spec(width, col_block=0):
        return pl.BlockSpec((rows, width), lambda i: (i, col_block))

    def full_spec(shape):
        return pl.BlockSpec(shape, lambda i: (0, 0), pipeline_mode=pl.Buffered(1))

    return pl.pallas_call(
        _out_proj_kernel,
        grid=(t // rows,),
        in_specs=[
            row_spec(GLA_V), row_spec(SB_W),
            row_spec(D_MODEL, OFF_M_A // D_MODEL), row_spec(D_MODEL, OFF_M_B // D_MODEL),
            row_spec(D_MODEL), row_spec(PLE_DIM),
            full_spec((GLA_V, D_MODEL)), full_spec((SB_W, D_MODEL)), full_spec((D_MODEL, D_MODEL)),
            full_spec((D_MODEL, D_MODEL)), full_spec((PLE_DIM, D_MODEL)),
            full_spec((1, D_MODEL)), full_spec((1, D_MODEL)),
        ],
        out_specs=row_spec(D_MODEL),
        out_shape=jax.ShapeDtypeStruct((t, D_MODEL), F32),
        compiler_params=pltpu.CompilerParams(
            dimension_semantics=("arbitrary",), vmem_limit_bytes=VMEM_LIMIT_BYTES),
        name="out_proj",
    )(ga, gb, z2, z2, x2, p2, wa, wb, wo, wg, wp, g_ple, g_final)


def _layer(layer, x, p, g_mix, w_in_stack, w_alpha, b_alpha, g_gla_out, w_out_gla, w_out_sb, w_out, g_ple,
           w_ple_gate, w_ple, g_last):
    bsz, seq, _ = x.shape
    t = bsz * seq
    x2 = x.reshape(t, D_MODEL)
    z, lr = _in_proj(x2, g_mix.reshape(1, D_MODEL), jnp.swapaxes(w_in_stack[layer], 0, 1).astype(BF16))
    z3 = z.reshape(bsz, seq, Z_COLS)
    w_alpha_p = jnp.pad(w_alpha, ((0, LANE - GLA_LOWRANK), (0, 0))).astype(BF16)
    ga = _gla(z3, lr.reshape(bsz, seq, LANE), w_alpha_p, b_alpha.reshape(1, GLA_QK),
              g_gla_out.reshape(1, GLA_V))
    gb = _stick_breaking(z3)
    out = _out_proj(ga.reshape(t, GLA_V), gb.reshape(t, SB_W), z, x2, p.reshape(t, PLE_DIM),
                    w_out_gla.astype(BF16), w_out_sb.astype(BF16), w_out.astype(BF16),
                    w_ple_gate.astype(BF16), w_ple.astype(BF16),
                    g_ple.reshape(1, D_MODEL), g_last.reshape(1, D_MODEL))
    return out.reshape(bsz, seq, D_MODEL)


def kernel(x, p, g_mix, w_in, w_alpha, b_alpha, g_gla_out, w_out_gla, w_out_sb, w_out, g_ple, w_ple_gate, w_ple,
           g_final):
    depth = p.shape[0]
    assert depth == 1, "the fused final RMSNorm assumes a single layer"
    return _layer(0, x, p[0], g_mix[0], w_in, w_alpha[0], b_alpha[0], g_gla_out[0], w_out_gla[0], w_out_sb[0],
                  w_out[0], g_ple[0], w_ple_gate[0], w_ple[0], g_final)
```

```python
import jax
import jax.numpy as jnp
from jax import lax
from jax.experimental import pallas as pl
from jax.experimental.pallas import tpu as pltpu

D_MODEL = 1024
CHUNK = 64
PLE_DIM = 256
GLA_HEADS = 4
GLA_DK = 128
GLA_DV = 256
GLA_LOWRANK = 16
GLA_TAU = 16.0
SB_HEADS = 8
SB_DH = 128
GLA_QK = GLA_HEADS * GLA_DK
GLA_V = GLA_HEADS * GLA_DV
SB_W = SB_HEADS * SB_DH
EPS = 1e-6

LANE = 128

OFF_QK_A = 0
OFF_V_A = OFF_QK_A + 2 * GLA_QK
OFF_GATE_A = OFF_V_A + GLA_V
OFF_Q_B = OFF_GATE_A + GLA_V
OFF_K_B = OFF_Q_B + SB_W
OFF_V_B = OFF_K_B + SB_W
OFF_GATE_B = OFF_V_B + SB_W
OFF_M_A = OFF_GATE_B + SB_W
OFF_M_B = OFF_M_A + D_MODEL
Z_COLS = OFF_M_B + D_MODEL

LOG2_E = 1.4426950408889634
SB_DEAD_DEFICIT = 104.0 * LOG2_E
SB_NO_KEYS_DEFICIT = 1e30
SB_STREAMS = 64

IN_PROJ_ROWS = 512
IN_PROJ_COLS = 1024
SB_Q_BLOCK = 64
SB_K_STEP = 256
OUT_PROJ_ROWS = 1024
GLA_ROWS = 512
GLA_ATTN_ROWS = 4 * CHUNK

VMEM_LIMIT_BYTES = 56 * 1024 * 1024

BF16 = jnp.bfloat16
F32 = jnp.float32


def _dot(a, b):
    return jnp.dot(a, b, preferred_element_type=F32)


def _dot_nt(a, b):
    return lax.dot_general(a, b, (((1,), (1,)), ((), ())), preferred_element_type=F32)


def _dot_tn(a, b):
    return lax.dot_general(a, b, (((0,), (0,)), ((), ())), preferred_element_type=F32)


def _rms_scale(x):
    return lax.rsqrt(jnp.mean(x * x, axis=-1, keepdims=True) + EPS)


def _sigmoid(g):
    return 1.0 / (1.0 + jnp.exp(-g))


def _silu(g):
    return g * _sigmoid(g)


def _in_proj_kernel(x_ref, g_ref, wt_ref, z_ref, lr_ref):
    x = x_ref[...]
    h = (x * _rms_scale(x) * g_ref[...]).astype(BF16)
    lr0 = OFF_GATE_A
    lr1 = lr0 + GLA_LOWRANK
    for w0, z0, width in ((0, 0, lr0), (lr1, lr0, Z_COLS - lr0)):
        for c in range(0, width, IN_PROJ_COLS):
            c1 = min(c + IN_PROJ_COLS, width)
            z_ref[:, z0 + c:z0 + c1] = _dot_nt(h, wt_ref[w0 + c:w0 + c1, :]).astype(BF16)
    lr_ref[...] = _dot_nt(h, wt_ref[lr0:lr0 + LANE, :])


def _in_proj(x2, g_mix, w_in_t):
    t = x2.shape[0]
    return pl.pallas_call(
        _in_proj_kernel,
        grid=(t // IN_PROJ_ROWS,),
        in_specs=[
            pl.BlockSpec((IN_PROJ_ROWS, D_MODEL), lambda i: (i, 0)),
            pl.BlockSpec((1, D_MODEL), lambda i: (0, 0)),
            pl.BlockSpec(w_in_t.shape, lambda i: (0, 0), pipeline_mode=pl.Buffered(1)),
        ],
        out_specs=[
            pl.BlockSpec((IN_PROJ_ROWS, Z_COLS), lambda i: (i, 0)),
            pl.BlockSpec((IN_PROJ_ROWS, LANE), lambda i: (i, 0)),
        ],
        out_shape=[
            jax.ShapeDtypeStruct((t, Z_COLS), BF16),
            jax.ShapeDtypeStruct((t, LANE), F32),
        ],
        compiler_params=pltpu.CompilerParams(
            dimension_semantics=("arbitrary",), vmem_limit_bytes=VMEM_LIMIT_BYTES),
        name="in_proj",
    )(x2, g_mix, w_in_t)


def _gla_kernel(qk_ref, v_ref, gate_ref, lr_ref, walpha_ref, balpha_ref, gout_ref, o_ref, state_ref):
    rows, arows = GLA_ROWS, GLA_ATTN_ROWS
    chunk_shift = CHUNK.bit_length() - 1

    @pl.when(pl.program_id(1) == 0)
    def _():
        state_ref[...] = jnp.zeros_like(state_ref)

    r_id = lax.broadcasted_iota(jnp.int32, (rows, rows), 0)
    c_id = lax.broadcasted_iota(jnp.int32, (rows, rows), 1)
    same_chunk = jnp.right_shift(r_id, chunk_shift) == jnp.right_shift(c_id, chunk_shift)
    cum_and_rest = jnp.concatenate([jnp.where(same_chunk & (r_id >= c_id), 1.0, 0.0),
                                    jnp.where(same_chunk & (r_id < c_id), 1.0, 0.0)], axis=0).astype(BF16)
    ar_id = lax.broadcasted_iota(jnp.int32, (arows, arows), 0)
    ac_id = lax.broadcasted_iota(jnp.int32, (arows, arows), 1)
    attn_same_chunk = jnp.right_shift(ar_id, chunk_shift) == jnp.right_shift(ac_id, chunk_shift)
    attn_causal = ar_id >= ac_id

    logits = _dot(lr_ref[...].astype(BF16), walpha_ref[...]) + balpha_ref[...]
    log_a = (jnp.minimum(logits, 0.0) - jnp.log(1.0 + jnp.exp(-jnp.abs(logits)))) * (1.0 / GLA_TAU)
    b_and_rest = _dot(cum_and_rest, log_a.astype(BF16))

    q_scale = GLA_DK ** -0.5
    for h in range(GLA_HEADS):
        ks = slice(h * GLA_DK, (h + 1) * GLA_DK)
        vs = slice(h * GLA_DV, (h + 1) * GLA_DV)
        q = qk_ref[:, h * GLA_DK:(h + 1) * GLA_DK]
        k = qk_ref[:, GLA_QK + h * GLA_DK:GLA_QK + (h + 1) * GLA_DK]
        v = v_ref[:, vs]
        b = b_and_rest[:rows, ks]
        rest = b_and_rest[rows:, ks]
        eb = jnp.exp(b)
        enb = jnp.exp(-b)
        qd = q * (eb * q_scale).astype(BF16)
        kd = k * enb.astype(BF16)
        qi = q * (enb * q_scale).astype(BF16)
        ki = k * eb.astype(BF16)
        k_end = k * jnp.exp(rest).astype(BF16)

        o_intra = []
        for a0 in range(0, rows, arows):
            sl = slice(a0, a0 + arows)
            a_past = _dot_nt(qd[sl], kd[sl])
            a_fut = _dot_nt(qi[sl], ki[sl])
            attn = jnp.where(attn_same_chunk, jnp.where(attn_causal, a_past, a_fut), 0.0)
            o_intra.append(_dot(attn.astype(BF16), v[sl]))

        n_chunks = rows // CHUNK
        in_first_chunk = lax.broadcasted_iota(jnp.int32, (2 * CHUNK, GLA_DK), 0) < CHUNK
        updates = []
        for c in range(0, n_chunks, 2):
            sl = slice(c * CHUNK, (c + 2) * CHUNK)
            ke = k_end[sl]
            zero = jnp.zeros_like(ke)
            both = _dot_tn(v[sl], jnp.concatenate([jnp.where(in_first_chunk, ke, zero),
                                                   jnp.where(in_first_chunk, zero, ke)], axis=1))
            updates += [both[:, :GLA_DK], both[:, GLA_DK:]]
        state = state_ref[h]
        states = []
        for c in range(n_chunks):
            c0 = c * CHUNK
            states.append(state.astype(BF16))
            decay = jnp.exp(b[c0:c0 + 1] + rest[c0:c0 + 1])
            state = state * decay + updates[c]
        state_ref[h] = state
        o_inter = []
        for c in range(0, n_chunks, 2):
            both = _dot_nt(qd[c * CHUNK:(c + 2) * CHUNK], jnp.concatenate(states[c:c + 2], axis=0))
            o_inter += [both[:CHUNK, :GLA_DV], both[CHUNK:, GLA_DV:]]

        o = jnp.concatenate(o_intra, axis=0) + jnp.concatenate(o_inter, axis=0)
        o = o * _rms_scale(o) * gout_ref[:, vs]
        o_ref[:, vs] = (o * _silu(gate_ref[:, vs].astype(F32))).astype(BF16)


def _gla(z3, lr3, w_alpha, b_alpha, g_out):
    bsz, seq, _ = z3.shape
    nblk = D_MODEL
    return pl.pallas_call(
        _gla_kernel,
        grid=(bsz, seq // GLA_ROWS),
        in_specs=[
            pl.BlockSpec((None, GLA_ROWS, 2 * GLA_QK), lambda b, s: (b, s, OFF_QK_A // nblk)),
            pl.BlockSpec((None, GLA_ROWS, GLA_V), lambda b, s: (b, s, OFF_V_A // nblk)),
            pl.BlockSpec((None, GLA_ROWS, GLA_V), lambda b, s: (b, s, OFF_GATE_A // nblk)),
            pl.BlockSpec((None, GLA_ROWS, LANE), lambda b, s: (b, s, 0)),
            pl.BlockSpec((LANE, GLA_QK), lambda b, s: (0, 0)),
            pl.BlockSpec((1, GLA_QK), lambda b, s: (0, 0)),
            pl.BlockSpec((1, GLA_V), lambda b, s: (0, 0)),
        ],
        out_specs=pl.BlockSpec((None, GLA_ROWS, GLA_V), lambda b, s: (b, s, 0)),
        out_shape=jax.ShapeDtypeStruct((bsz, seq, GLA_V), BF16),
        scratch_shapes=[pltpu.VMEM((GLA_HEADS, GLA_DV, GLA_DK), F32)],
        compiler_params=pltpu.CompilerParams(
            dimension_semantics=("arbitrary", "arbitrary"), vmem_limit_bytes=VMEM_LIMIT_BYTES),
        name="gla",
    )(z3, z3, z3, lr3, w_alpha, b_alpha, g_out)


def _sb_kernel(q_ref, k_ref, v_ref, gate_ref, o_ref, acc_ref, spent_ref):
    seq = q_ref.shape[0]
    tq, kstep, n_streams = SB_Q_BLOCK, SB_K_STEP, SB_STREAMS
    lookback = kstep - tq
    to_log2_logit = (SB_DH ** -0.5) * LOG2_E
    r_id = lax.broadcasted_iota(jnp.int32, (kstep, kstep), 0)
    c_id = lax.broadcasted_iota(jnp.int32, (kstep, kstep), 1)
    later_keys = jnp.where(r_id > c_id, 1.0, 0.0).astype(BF16)
    lane_id = lax.broadcasted_iota(jnp.int32, (tq, kstep), 1)
    strictly_before = lane_id < lax.broadcasted_iota(jnp.int32, (tq, kstep), 0) + lookback

    def window(i0, t, first, every_stream_has_keys):
        log2_betas, deficits, spents, vs, masks, k0s = [], [], [], [], [], []
        for s in range(n_streams):
            q0 = pl.multiple_of((i0 + s) * tq, tq)
            k0 = (i0 + s) * tq - lookback - t * kstep
            k0s.append(k0)
            if every_stream_has_keys:
                k0 = pl.multiple_of(k0, tq)
                k, v = k_ref[pl.ds(k0, kstep), :], v_ref[pl.ds(k0, kstep), :]
                mask = strictly_before if first else None
                spent = None if first else spent_ref[s]
            else:
                starts = [pl.multiple_of(jnp.maximum(k0 + off, 0), tq) for off in range(0, kstep, tq)]
                k = jnp.concatenate([k_ref[pl.ds(st, tq), :] for st in starts], axis=0)
                v = jnp.concatenate([v_ref[pl.ds(st, tq), :] for st in starts], axis=0)
                mask = lane_id >= -k0
                if first:
                    mask = jnp.logical_and(mask, strictly_before)
                spent = None if first else jnp.where(k0 + kstep <= 0, SB_NO_KEYS_DEFICIT, spent_ref[s])
            u = _dot_nt(q_ref[pl.ds(q0, tq), :], k) * to_log2_logit
            deficit = jnp.maximum(u, 0.0) + jnp.log2(1.0 + 1.0 / jnp.exp2(jnp.abs(u)))
            log2_betas.append(u - deficit)
            deficits.append(deficit if mask is None else jnp.where(mask, deficit, 0.0))
            spents.append(spent)
            vs.append(v)
            masks.append(mask)
        laters = []
        for s in range(0, n_streams, 2):
            pair = _dot(jnp.concatenate([deficits[s], deficits[s + 1]], axis=0).astype(BF16), later_keys)
            laters += [pair[:tq], pair[tq:]]
        least_spent = None
        for s in range(n_streams):
            exponent = log2_betas[s] - laters[s]
            if not first:
                exponent = exponent - jnp.concatenate([spents[s]] * (kstep // LANE), axis=1)
            w = jnp.exp2(exponent)
            if masks[s] is not None:
                w = jnp.where(masks[s], w, 0.0)
            pv = _dot(w.astype(BF16), vs[s])
            total = jnp.sum(deficits[s], axis=1, keepdims=True)
            if first:
                acc_ref[s] = pv
                spent = jnp.broadcast_to(total, (tq, LANE))
            else:
                acc_ref[s] += pv
                spent = spents[s] + total
            if not every_stream_has_keys:
                spent = jnp.where(k0s[s] <= 0, SB_NO_KEYS_DEFICIT, spent)
            spent_ref[s] = spent
            least_spent = spent if least_spent is None else jnp.minimum(least_spent, spent)
        return jnp.min(least_spent)

    def q_group(g, _):
        i0 = g * n_streams
        least_spent = lax.cond(i0 * tq >= lookback,
                               lambda: window(i0, 0, True, True),
                               lambda: window(i0, 0, True, False))

        def alive(st):
            return st[1] < SB_DEAD_DEFICIT

        def body(st):
            return st[0] + 1, window(i0, st[0], False, False)

        lax.while_loop(alive, body, (jnp.int32(1), least_spent))
        for s in range(n_streams):
            q0 = pl.multiple_of((i0 + s) * tq, tq)
            gate = gate_ref[pl.ds(q0, tq), :].astype(F32)
            o_ref[pl.ds(q0, tq), :] = (acc_ref[s] * _silu(gate)).astype(BF16)
        return 0

    lax.fori_loop(0, seq // (tq * n_streams), q_group, 0)


def _stick_breaking(z3):
    bsz, seq, _ = z3.shape

    def spec(off):
        return pl.BlockSpec((None, seq, SB_DH), lambda b, h: (b, 0, off // SB_DH + h))

    return pl.pallas_call(
        _sb_kernel,
        grid=(bsz, SB_HEADS),
        in_specs=[spec(OFF_Q_B), spec(OFF_K_B), spec(OFF_V_B), spec(OFF_GATE_B)],
        out_specs=pl.BlockSpec((None, seq, SB_DH), lambda b, h: (b, 0, h)),
        out_shape=jax.ShapeDtypeStruct((bsz, seq, SB_W), BF16),
        scratch_shapes=[pltpu.VMEM((SB_STREAMS, SB_Q_BLOCK, SB_DH), F32),
                        pltpu.VMEM((SB_STREAMS, SB_Q_BLOCK, LANE), F32)],
        compiler_params=pltpu.CompilerParams(
            dimension_semantics=("arbitrary", "arbitrary"), vmem_limit_bytes=VMEM_LIMIT_BYTES),
        name="stickbrk",
    )(z3, z3, z3, z3)


def _out_proj_kernel(ga_ref, gb_ref, ma_ref, mb_ref, x_ref, p_ref, wa_ref, wb_ref, wo_ref, wg_ref, wp_ref,
                     gple_ref, gfin_ref, o_ref):
    y_a = _dot(ga_ref[...], wa_ref[...])
    y_b = _dot(gb_ref[...], wb_ref[...])
    merged = _sigmoid(ma_ref[...].astype(F32)) * y_a + _sigmoid(mb_ref[...].astype(F32)) * y_b
    x1 = x_ref[...] + _dot(merged.astype(BF16), wo_ref[...])
    u = (x1 * _rms_scale(x1) * gple_ref[...]).astype(BF16)
    gate_p = _sigmoid(_dot(u, wg_ref[...]))
    x2 = x1 + gate_p * _dot(p_ref[...].astype(BF16), wp_ref[...])
    o_ref[...] = x2 * _rms_scale(x2) * gfin_ref[...]


def _out_proj(ga, gb, z2, x2, p2, wa, wb, wo, wg, wp, g_ple, g_final):
    t = x2.shape[0]
    rows = OUT_PROJ_ROWS

    def row_spec(width, col_block=0):
        return pl.BlockSpec((rows, width), lambda i: (i, col_block))

    def full_spec(shape):
        return pl.BlockSpec(shape, lambda i: (0, 0), pipeline_mode=pl.Buffered(1))

    return pl.pallas_call(
        _out_proj_kernel,
        grid=(t // rows,),
        in_specs=[
            row_spec(GLA_V), row_spec(SB_W),
            row_spec(D_MODEL, OFF_M_A // D_MODEL), row_spec(D_MODEL, OFF_M_B // D_MODEL),
            row_spec(D_MODEL), row_spec(PLE_DIM),
            full_spec((GLA_V, D_MODEL)), full_spec((SB_W, D_MODEL)), full_spec((D_MODEL, D_MODEL)),
            full_spec((D_MODEL, D_MODEL)), full_spec((PLE_DIM, D_MODEL)),
            full_spec((1, D_MODEL)), full_spec((1, D_MODEL)),
        ],
        out_specs=row_spec(D_MODEL),
        out_shape=jax.ShapeDtypeStruct((t, D_MODEL), F32),
        compiler_params=pltpu.CompilerParams(
            dimension_semantics=("arbitrary",), vmem_limit_bytes=VMEM_LIMIT_BYTES),
        name="out_proj",
    )(ga, gb, z2, z2, x2, p2, wa, wb, wo, wg, wp, g_ple, g_final)


def _layer(layer, x, p, g_mix, w_in_stack, w_alpha, b_alpha, g_gla_out, w_out_gla, w_out_sb, w_out, g_ple,
           w_ple_gate, w_ple, g_last):
    bsz, seq, _ = x.shape
    t = bsz * seq
    x2 = x.reshape(t, D_MODEL)
    z, lr = _in_proj(x2, g_mix.reshape(1, D_MODEL), jnp.swapaxes(w_in_stack[layer], 0, 1).astype(BF16))
    z3 = z.reshape(bsz, seq, Z_COLS)
    w_alpha_p = jnp.pad(w_alpha, ((0, LANE - GLA_LOWRANK), (0, 0))).astype(BF16)
    ga = _gla(z3, lr.reshape(bsz, seq, LANE), w_alpha_p, b_alpha.reshape(1, GLA_QK),
              g_gla_out.reshape(1, GLA_V))
    gb = _stick_breaking(z3)
    out = _out_proj(ga.reshape(t, GLA_V), gb.reshape(t, SB_W), z, x2, p.reshape(t, PLE_DIM),
                    w_out_gla.astype(BF16), w_out_sb.astype(BF16), w_out.astype(BF16),
                    w_ple_gate.astype(BF16), w_ple.astype(BF16),
                    g_ple.reshape(1, D_MODEL), g_last.reshape(1, D_MODEL))
    return out.reshape(bsz, seq, D_MODEL)


def kernel(x, p, g_mix, w_in, w_alpha, b_alpha, g_gla_out, w_out_gla, w_out_sb, w_out, g_ple, w_ple_gate, w_ple,
           g_final):
    depth = p.shape[0]
    assert depth == 1, "the fused final RMSNorm assumes a single layer"
    return _layer(0, x, p[0], g_mix[0], w_in, w_alpha[0], b_alpha[0], g_gla_out[0], w_out_gla[0], w_out_sb[0],
                  w_out[0], g_ple[0], w_ple_gate[0], w_ple[0], g_final)
```

```python
import jax
import jax.numpy as jnp
from jax import lax
from jax.experimental import pallas as pl
from jax.experimental.pallas import tpu as pltpu

D_MODEL = 1024
CHUNK = 64
PLE_DIM = 256
GLA_HEADS = 4
GLA_DK = 128
GLA_DV = 256
GLA_LOWRANK = 16
GLA_TAU = 16.0
SB_HEADS = 8
SB_DH = 128
GLA_QK = GLA_HEADS * GLA_DK
GLA_V = GLA_HEADS * GLA_DV
SB_W = SB_HEADS * SB_DH
EPS = 1e-6

LANE = 128

OFF_QK_A = 0
OFF_V_A = OFF_QK_A + 2 * GLA_QK
OFF_GATE_A = OFF_V_A + GLA_V
OFF_Q_B = OFF_GATE_A + GLA_V
OFF_K_B = OFF_Q_B + SB_W
OFF_V_B = OFF_K_B + SB_W
OFF_GATE_B = OFF_V_B + SB_W
OFF_M_A = OFF_GATE_B + SB_W
OFF_M_B = OFF_M_A + D_MODEL
Z_COLS = OFF_M_B + D_MODEL

LOG2_E = 1.4426950408889634
SB_DEAD_DEFICIT = 104.0 * LOG2_E
SB_NO_KEYS_DEFICIT = 1e30
SB_STREAMS = 64

IN_PROJ_ROWS = 512
IN_PROJ_COLS = 1024
SB_Q_BLOCK = 64
SB_K_STEP = 256
OUT_PROJ_ROWS = 1024
GLA_ROWS = 512
GLA_ATTN_ROWS = 4 * CHUNK

VMEM_LIMIT_BYTES = 56 * 1024 * 1024

BF16 = jnp.bfloat16
F32 = jnp.float32


def _dot(a, b):
    return jnp.dot(a, b, preferred_element_type=F32)


def _dot_nt(a, b):
    return lax.dot_general(a, b, (((1,), (1,)), ((), ())), preferred_element_type=F32)


def _dot_tn(a, b):
    return lax.dot_general(a, b, (((0,), (0,)), ((), ())), preferred_element_type=F32)


def _rms_scale(x):
    return lax.rsqrt(jnp.mean(x * x, axis=-1, keepdims=True) + EPS)


def _sigmoid(g):
    return 1.0 / (1.0 + jnp.exp(-g))


def _silu(g):
    return g * _sigmoid(g)


def _in_proj_kernel(x_ref, g_ref, wt_ref, z_ref, lr_ref, h_ref):
    i = pl.program_id(0)

    @pl.when(i == 0)
    def _():
        h_ref[...] = jnp.zeros_like(h_ref)

    slot = i % 2
    x = x_ref[...]
    h_ref[slot] = (x * _rms_scale(x) * g_ref[...]).astype(BF16)
    h = h_ref[1 - slot]
    lr0 = OFF_GATE_A
    lr1 = lr0 + GLA_LOWRANK
    for w0, z0, width in ((0, 0, lr0), (lr1, lr0, Z_COLS - lr0)):
        for c in range(0, width, IN_PROJ_COLS):
            c1 = min(c + IN_PROJ_COLS, width)
            z_ref[:, z0 + c:z0 + c1] = _dot_nt(h, wt_ref[w0 + c:w0 + c1, :]).astype(BF16)
    lr_ref[...] = _dot_nt(h, wt_ref[lr0:lr0 + LANE, :])


def _in_proj(x2, g_mix, w_in_t):
    t = x2.shape[0]
    n_tiles = t // IN_PROJ_ROWS
    return pl.pallas_call(
        _in_proj_kernel,
        grid=(n_tiles + 1,),
        in_specs=[
            pl.BlockSpec((IN_PROJ_ROWS, D_MODEL), lambda i: (jnp.minimum(i, n_tiles - 1), 0)),
            pl.BlockSpec((1, D_MODEL), lambda i: (0, 0)),
            pl.BlockSpec(w_in_t.shape, lambda i: (0, 0), pipeline_mode=pl.Buffered(1)),
        ],
        out_specs=[
            pl.BlockSpec((IN_PROJ_ROWS, Z_COLS), lambda i: (jnp.maximum(i - 1, 0), 0)),
            pl.BlockSpec((IN_PROJ_ROWS, LANE), lambda i: (jnp.maximum(i - 1, 0), 0)),
        ],
        out_shape=[
            jax.ShapeDtypeStruct((t, Z_COLS), BF16),
            jax.ShapeDtypeStruct((t, LANE), F32),
        ],
        scratch_shapes=[pltpu.VMEM((2, IN_PROJ_ROWS, D_MODEL), BF16)],
        compiler_params=pltpu.CompilerParams(
            dimension_semantics=("arbitrary",), vmem_limit_bytes=VMEM_LIMIT_BYTES),
        name="in_proj",
    )(x2, g_mix, w_in_t)


def _gla_kernel(qk_ref, v_ref, gate_ref, lr_ref, walpha_ref, balpha_ref, gout_ref, o_ref, state_ref):
    rows, arows = GLA_ROWS, GLA_ATTN_ROWS
    chunk_shift = CHUNK.bit_length() - 1

    @pl.when(pl.program_id(1) == 0)
    def _():
        state_ref[...] = jnp.zeros_like(state_ref)

    r_id = lax.broadcasted_iota(jnp.int32, (rows, rows), 0)
    c_id = lax.broadcasted_iota(jnp.int32, (rows, rows), 1)
    same_chunk = jnp.right_shift(r_id, chunk_shift) == jnp.right_shift(c_id, chunk_shift)
    cum_and_rest = jnp.concatenate([jnp.where(same_chunk & (r_id >= c_id), 1.0, 0.0),
                                    jnp.where(same_chunk & (r_id < c_id), 1.0, 0.0)], axis=0).astype(BF16)
    ar_id = lax.broadcasted_iota(jnp.int32, (arows, arows), 0)
    ac_id = lax.broadcasted_iota(jnp.int32, (arows, arows), 1)
    attn_same_chunk = jnp.right_shift(ar_id, chunk_shift) == jnp.right_shift(ac_id, chunk_shift)
    attn_causal = ar_id >= ac_id

    logits = _dot(lr_ref[...].astype(BF16), walpha_ref[...]) + balpha_ref[...]
    log_a = (jnp.minimum(logits, 0.0) - jnp.log(1.0 + jnp.exp(-jnp.abs(logits)))) * (1.0 / GLA_TAU)
    b_and_rest = _dot(cum_and_rest, log_a.astype(BF16))

    q_scale = GLA_DK ** -0.5
    for h in range(GLA_HEADS):
        ks = slice(h * GLA_DK, (h + 1) * GLA_DK)
        vs = slice(h * GLA_DV, (h + 1) * GLA_DV)
        q = qk_ref[:, h * GLA_DK:(h + 1) * GLA_DK]
        k = qk_ref[:, GLA_QK + h * GLA_DK:GLA_QK + (h + 1) * GLA_DK]
        v = v_ref[:, vs]
        b = b_and_rest[:rows, ks]
        rest = b_and_rest[rows:, ks]
        eb = jnp.exp(b)
        enb = jnp.exp(-b)
        qd = q * (eb * q_scale).astype(BF16)
        kd = k * enb.astype(BF16)
        qi = q * (enb * q_scale).astype(BF16)
        ki = k * eb.astype(BF16)
        k_end = k * jnp.exp(rest).astype(BF16)

        o_intra = []
        for a0 in range(0, rows, arows):
            sl = slice(a0, a0 + arows)
            a_past = _dot_nt(qd[sl], kd[sl])
            a_fut = _dot_nt(qi[sl], ki[sl])
            attn = jnp.where(attn_same_chunk, jnp.where(attn_causal, a_past, a_fut), 0.0)
            o_intra.append(_dot(attn.astype(BF16), v[sl]))

        n_chunks = rows // CHUNK
        in_first_chunk = lax.broadcasted_iota(jnp.int32, (2 * CHUNK, GLA_DK), 0) < CHUNK
        updates = []
        for c in range(0, n_chunks, 2):
            sl = slice(c * CHUNK, (c + 2) * CHUNK)
            ke = k_end[sl]
            zero = jnp.zeros_like(ke)
            both = _dot_tn(v[sl], jnp.concatenate([jnp.where(in_first_chunk, ke, zero),
                                                   jnp.where(in_first_chunk, zero, ke)], axis=1))
            updates += [both[:, :GLA_DK], both[:, GLA_DK:]]
        state = state_ref[h]
        states = []
        for c in range(n_chunks):
            c0 = c * CHUNK
            states.append(state.astype(BF16))
            decay = jnp.exp(b[c0:c0 + 1] + rest[c0:c0 + 1])
            state = state * decay + updates[c]
        state_ref[h] = state
        o_inter = []
        for c in range(0, n_chunks, 2):
            both = _dot_nt(qd[c * CHUNK:(c + 2) * CHUNK], jnp.concatenate(states[c:c + 2], axis=0))
            o_inter += [both[:CHUNK, :GLA_DV], both[CHUNK:, GLA_DV:]]

        o = jnp.concatenate(o_intra, axis=0) + jnp.concatenate(o_inter, axis=0)
        o = o * _rms_scale(o) * gout_ref[:, vs]
        o_ref[:, vs] = (o * _silu(gate_ref[:, vs].astype(F32))).astype(BF16)


def _gla(z3, lr3, w_alpha, b_alpha, g_out):
    bsz, seq, _ = z3.shape
    nblk = D_MODEL
    return pl.pallas_call(
        _gla_kernel,
        grid=(bsz, seq // GLA_ROWS),
        in_specs=[
            pl.BlockSpec((None, GLA_ROWS, 2 * GLA_QK), lambda b, s: (b, s, OFF_QK_A // nblk)),
            pl.BlockSpec((None, GLA_ROWS, GLA_V), lambda b, s: (b, s, OFF_V_A // nblk)),
            pl.BlockSpec((None, GLA_ROWS, GLA_V), lambda b, s: (b, s, OFF_GATE_A // nblk)),
            pl.BlockSpec((None, GLA_ROWS, LANE), lambda b, s: (b, s, 0)),
            pl.BlockSpec((LANE, GLA_QK), lambda b, s: (0, 0)),
            pl.BlockSpec((1, GLA_QK), lambda b, s: (0, 0)),
            pl.BlockSpec((1, GLA_V), lambda b, s: (0, 0)),
        ],
        out_specs=pl.BlockSpec((None, GLA_ROWS, GLA_V), lambda b, s: (b, s, 0)),
        out_shape=jax.ShapeDtypeStruct((bsz, seq, GLA_V), BF16),
        scratch_shapes=[pltpu.VMEM((GLA_HEADS, GLA_DV, GLA_DK), F32)],
        compiler_params=pltpu.CompilerParams(
            dimension_semantics=("arbitrary", "arbitrary"), vmem_limit_bytes=VMEM_LIMIT_BYTES),
        name="gla",
    )(z3, z3, z3, lr3, w_alpha, b_alpha, g_out)


def _sb_kernel(q_ref, k_ref, v_ref, gate_ref, o_ref, acc_ref, spent_ref):
    seq = q_ref.shape[0]
    tq, kstep, n_streams = SB_Q_BLOCK, SB_K_STEP, SB_STREAMS
    lookback = kstep - tq
    to_log2_logit = (SB_DH ** -0.5) * LOG2_E
    r_id = lax.broadcasted_iota(jnp.int32, (kstep, kstep), 0)
    c_id = lax.broadcasted_iota(jnp.int32, (kstep, kstep), 1)
    later_keys = jnp.where(r_id > c_id, 1.0, 0.0).astype(BF16)
    lane_id = lax.broadcasted_iota(jnp.int32, (tq, kstep), 1)
    strictly_before = lane_id < lax.broadcasted_iota(jnp.int32, (tq, kstep), 0) + lookback

    def window(i0, t, first, every_stream_has_keys):
        log2_betas, deficits, spents, vs, masks, k0s = [], [], [], [], [], []
        for s in range(n_streams):
            q0 = pl.multiple_of((i0 + s) * tq, tq)
            k0 = (i0 + s) * tq - lookback - t * kstep
            k0s.append(k0)
            if every_stream_has_keys:
                k0 = pl.multiple_of(k0, tq)
                k, v = k_ref[pl.ds(k0, kstep), :], v_ref[pl.ds(k0, kstep), :]
                mask = strictly_before if first else None
                spent = None if first else spent_ref[s]
            else:
                starts = [pl.multiple_of(jnp.maximum(k0 + off, 0), tq) for off in range(0, kstep, tq)]
                k = jnp.concatenate([k_ref[pl.ds(st, tq), :] for st in starts], axis=0)
                v = jnp.concatenate([v_ref[pl.ds(st, tq), :] for st in starts], axis=0)
                mask = lane_id >= -k0
                if first:
                    mask = jnp.logical_and(mask, strictly_before)
                spent = None if first else jnp.where(k0 + kstep <= 0, SB_NO_KEYS_DEFICIT, spent_ref[s])
            u = _dot_nt(q_ref[pl.ds(q0, tq), :], k) * to_log2_logit
            deficit = jnp.maximum(u, 0.0) + jnp.log2(1.0 + 1.0 / jnp.exp2(jnp.abs(u)))
            log2_betas.append(u - deficit)
            deficits.append(deficit if mask is None else jnp.where(mask, deficit, 0.0))
            spents.append(spent)
            vs.append(v)
            masks.append(mask)
        laters = []
        for s in range(0, n_streams, 2):
            pair = _dot(jnp.concatenate([deficits[s], deficits[s + 1]], axis=0).astype(BF16), later_keys)
            laters += [pair[:tq], pair[tq:]]
        least_spent = None
        for s in range(n_streams):
            exponent = log2_betas[s] - laters[s]
            if not first:
                exponent = exponent - jnp.concatenate([spents[s]] * (kstep // LANE), axis=1)
            w = jnp.exp2(exponent)
            if masks[s] is not None:
                w = jnp.where(masks[s], w, 0.0)
            pv = _dot(w.astype(BF16), vs[s])
            total = jnp.sum(deficits[s], axis=1, keepdims=True)
            if first:
                acc_ref[s] = pv
                spent = jnp.broadcast_to(total, (tq, LANE))
            else:
                acc_ref[s] += pv
                spent = spents[s] + total
            if not every_stream_has_keys:
                spent = jnp.where(k0s[s] <= 0, SB_NO_KEYS_DEFICIT, spent)
            spent_ref[s] = spent
            least_spent = spent if least_spent is None else jnp.minimum(least_spent, spent)
        return jnp.min(least_spent)

    def q_group(g, _):
        i0 = g * n_streams
        least_spent = lax.cond(i0 * tq >= lookback,
                               lambda: window(i0, 0, True, True),
                               lambda: window(i0, 0, True, False))

        def alive(st):
            return st[1] < SB_DEAD_DEFICIT

        def body(st):
            return st[0] + 1, window(i0, st[0], False, False)

        lax.while_loop(alive, body, (jnp.int32(1), least_spent))
        for s in range(n_streams):
            q0 = pl.multiple_of((i0 + s) * tq, tq)
            gate = gate_ref[pl.ds(q0, tq), :].astype(F32)
            o_ref[pl.ds(q0, tq), :] = (acc_ref[s] * _silu(gate)).astype(BF16)
        return 0

    lax.fori_loop(0, seq // (tq * n_streams), q_group, 0)


def _stick_breaking(z3):
    bsz, seq, _ = z3.shape

    def spec(off):
        return pl.BlockSpec((None, seq, SB_DH), lambda b, h: (b, 0, off // SB_DH + h))

    return pl.pallas_call(
        _sb_kernel,
        grid=(bsz, SB_HEADS),
        in_specs=[spec(OFF_Q_B), spec(OFF_K_B), spec(OFF_V_B), spec(OFF_GATE_B)],
        out_specs=pl.BlockSpec((None, seq, SB_DH), lambda b, h: (b, 0, h)),
        out_shape=jax.ShapeDtypeStruct((bsz, seq, SB_W), BF16),
        scratch_shapes=[pltpu.VMEM((SB_STREAMS, SB_Q_BLOCK, SB_DH), F32),
                        pltpu.VMEM((SB_STREAMS, SB_Q_BLOCK, LANE), F32)],
        compiler_params=pltpu.CompilerParams(
            dimension_semantics=("arbitrary", "arbitrary"), vmem_limit_bytes=VMEM_LIMIT_BYTES),
        name="stickbrk",
    )(z3, z3, z3, z3)


def _out_proj_kernel(ga_ref, gb_ref, ma_ref, mb_ref, x_ref, p_ref, wa_ref, wb_ref, wo_ref, wg_ref, wp_ref,
                     gple_ref, gfin_ref, o_ref):
    y_a = _dot(ga_ref[...], wa_ref[...])
    y_b = _dot(gb_ref[...], wb_ref[...])
    merged = _sigmoid(ma_ref[...].astype(F32)) * y_a + _sigmoid(mb_ref[...].astype(F32)) * y_b
    x1 = x_ref[...] + _dot(merged.astype(BF16), wo_ref[...])
    u = (x1 * _rms_scale(x1) * gple_ref[...]).astype(BF16)
    gate_p = _sigmoid(_dot(u, wg_ref[...]))
    x2 = x1 + gate_p * _dot(p_ref[...].astype(BF16), wp_ref[...])
    o_ref[...] = x2 * _rms_scale(x2) * gfin_ref[...]


def _out_proj(ga, gb, z2, x2, p2, wa, wb, wo, wg, wp, g_ple, g_final):
    t = x2.shape[0]
    rows = OUT_PROJ_ROWS

    def row_spec(width, col_block=0):
        return pl.BlockSpec((rows, width), lambda i: (i, col_block))

    def full_spec(shape):
        return pl.BlockSpec(shape, lambda i: (0, 0), pipeline_mode=pl.Buffered(1))

    return pl.pallas_call(
        _out_proj_kernel,
        grid=(t // rows,),
        in_specs=[
            row_spec(GLA_V), row_spec(SB_W),
            row_spec(D_MODEL, OFF_M_A // D_MODEL), row_spec(D_MODEL, OFF_M_B // D_MODEL),
            row_spec(D_MODEL), row_spec(PLE_DIM),
            full_spec((GLA_V, D_MODEL)), full_spec((SB_W, D_MODEL)), full_spec((D_MODEL, D_MODEL)),
            full_spec((D_MODEL, D_MODEL)), full_spec((PLE_DIM, D_MODEL)),
            full_spec((1, D_MODEL)), full_spec((1, D_MODEL)),
        ],
        out_specs=row_spec(D_MODEL),
        out_shape=jax.ShapeDtypeStruct((t, D_MODEL), F32),
        compiler_params=pltpu.CompilerParams(
            dimension_semantics=("arbitrary",), vmem_limit_bytes=VMEM_LIMIT_BYTES),
        name="out_proj",
    )(ga, gb, z2, z2, x2, p2, wa, wb, wo, wg, wp, g_ple, g_final)


def _layer(layer, x, p, g_mix, w_in_stack, w_alpha, b_alpha, g_gla_out, w_out_gla, w_out_sb, w_out, g_ple,
           w_ple_gate, w_ple, g_last):
    bsz, seq, _ = x.shape
    t = bsz * seq
    x2 = x.reshape(t, D_MODEL)
    z, lr = _in_proj(x2, g_mix.reshape(1, D_MODEL), jnp.swapaxes(w_in_stack[layer], 0, 1).astype(BF16))
    z3 = z.reshape(bsz, seq, Z_COLS)
    w_alpha_p = jnp.pad(w_alpha, ((0, LANE - GLA_LOWRANK), (0, 0))).astype(BF16)
    ga = _gla(z3, lr.reshape(bsz, seq, LANE), w_alpha_p, b_alpha.reshape(1, GLA_QK),
              g_gla_out.reshape(1, GLA_V))
    gb = _stick_breaking(z3)
    out = _out_proj(ga.reshape(t, GLA_V), gb.reshape(t, SB_W), z, x2, p.reshape(t, PLE_DIM),
                    w_out_gla.astype(BF16), w_out_sb.astype(BF16), w_out.astype(BF16),
                    w_ple_gate.astype(BF16), w_ple.astype(BF16),
                    g_ple.reshape(1, D_MODEL), g_last.reshape(1, D_MODEL))
    return out.reshape(bsz, seq, D_MODEL)


def kernel(x, p, g_mix, w_in, w_alpha, b_alpha, g_gla_out, w_out_gla, w_out_sb, w_out, g_ple, w_ple_gate, w_ple,
           g_final):
    depth = p.shape[0]
    assert depth == 1, "the fused final RMSNorm assumes a single layer"
    return _layer(0, x, p[0], g_mix[0], w_in, w_alpha[0], b_alpha[0], g_gla_out[0], w_out_gla[0], w_out_sb[0],
                  w_out[0], g_ple[0], w_ple_gate[0], w_ple[0], g_final)
```
